```python
import math
import jax, jax.numpy as jnp
from jax import lax
import numpy as np

D_MODEL = 1024
BATCH = 2
SEQ = 8192
DEPTH = 4
DEC_BATCH = 128
DEC_SEQ = 4
PAST_LEN = 2048
PAGE_SIZE = 128

N_META = 16
ATTN_WIDTH = D_MODEL // 2
POOL_WIDTH = D_MODEL - ATTN_WIDTH
HEAD_DIM = 64
V_DIM = 2 * HEAD_DIM
N_HEADS = ATTN_WIDTH // V_DIM
QK_WIDTH = N_HEADS * 2 * HEAD_DIM
POOL_WINDOWS = (2, 4, 8, 16)
N_POOL_GROUPS = len(POOL_WINDOWS)
POOL_GROUP = POOL_WIDTH // N_POOL_GROUPS
POOL_HIST = max(POOL_WINDOWS) - 1
PROJ_WIDTH = 2 * QK_WIDTH + ATTN_WIDTH + POOL_WIDTH
D_FF = -(-8 * D_MODEL // (3 * 256)) * 256
BLOCK_Q = 128
DEEPNORM_ALPHA = (2 * DEPTH) ** 0.25
DEEPNORM_BETA = (8 * DEPTH) ** -0.25
LN_EPS = 1e-5
SUBLN_EPS = 1e-5

kernel_name = 'hymba_diffattn_pool_deepnorm_step'


def layer_norm(x, g, b):
    xf = x.astype(jnp.float32)
    mu = jnp.mean(xf, axis=-1, keepdims=True)
    var = jnp.mean(jnp.square(xf - mu), axis=-1, keepdims=True)
    y = (xf - mu) * lax.rsqrt(var + LN_EPS) * g.astype(jnp.float32) + b.astype(jnp.float32)
    return y.astype(x.dtype)


def split_proj(h, w):
    p = jnp.einsum('ntd,de->nte', h, w)
    n, t = p.shape[0], p.shape[1]
    q = p[..., :QK_WIDTH].reshape(n, t, N_HEADS, 2, HEAD_DIM)
    k = p[..., QK_WIDTH:2 * QK_WIDTH].reshape(n, t, N_HEADS, 2, HEAD_DIM)
    v = p[..., 2 * QK_WIDTH:2 * QK_WIDTH + ATTN_WIDTH].reshape(n, t, N_HEADS, V_DIM)
    u = p[..., 2 * QK_WIDTH + ATTN_WIDTH:]
    return q, k, v, u


def diff_lambda(lq, lam_init):
    lq = lq.astype(jnp.float32)
    return jnp.exp(jnp.sum(lq[0] * lq[1])) - jnp.exp(jnp.sum(lq[2] * lq[3])) + lam_init


def diff_attn_core(q, k, v, mask, lam):
    s = jnp.einsum('nqhcd,nkhcd->nhcqk', q, k).astype(jnp.float32) * (HEAD_DIM ** -0.5)
    s = jnp.where(mask, s, -jnp.inf)
    p = jax.nn.softmax(s, axis=-1)
    w = p[:, :, 0] - lam * p[:, :, 1]
    return jnp.einsum('nhqk,nkhe->nqhe', w.astype(v.dtype), v)


def head_norm(o, g, lam_init):
    of = o.astype(jnp.float32)
    of = of * lax.rsqrt(jnp.mean(jnp.square(of), axis=-1, keepdims=True) + SUBLN_EPS)
    of = of * g.astype(jnp.float32) * (1.0 - lam_init)
    return of.astype(o.dtype).reshape(o.shape[0], o.shape[1], ATTN_WIDTH)


def prompt_attention(q, k, v, lam):
    n, L = q.shape[0], q.shape[1]
    n_blk = -(-L // BLOCK_Q)
    Lp = n_blk * BLOCK_Q
    qp = jnp.pad(q, ((0, 0), (0, Lp - L), (0, 0), (0, 0), (0, 0)))
    kp = jnp.pad(k, ((0, 0), (0, Lp - L), (0, 0), (0, 0), (0, 0)))
    vp = jnp.pad(v, ((0, 0), (0, Lp - L), (0, 0), (0, 0)))
    kpos = jnp.arange(Lp)

    def block(i):
        q_blk = lax.dynamic_slice_in_dim(qp, i * BLOCK_Q, BLOCK_Q, axis=1)
        qpos = i * BLOCK_Q + jnp.arange(BLOCK_Q)
        mask = kpos[None, :] <= qpos[:, None]
        return diff_attn_core(q_blk, kp, vp, mask, lam)

    o = lax.map(block, jnp.arange(n_blk))
    o = jnp.moveaxis(o, 0, 1).reshape(n, Lp, N_HEADS, V_DIM)
    return o[:, :L]


def sample_attention(q, k_new, v_new, cache_k_l, cache_v_l, page_table, lam):
    nb, t = q.shape[0], q.shape[1]
    kc = cache_k_l[page_table]
    past = kc.shape[1] * kc.shape[2]
    kc = kc.reshape(nb, past, N_HEADS, 2, HEAD_DIM)
    vc = cache_v_l[page_table].reshape(nb, past, N_HEADS, V_DIM)
    k_all = jnp.concatenate([kc, k_new], axis=1)
    v_all = jnp.concatenate([vc, v_new], axis=1)
    kj = jnp.arange(past + t)
    qi = past + jnp.arange(t)
    mask = kj[None, :] <= qi[:, None]
    return diff_attn_core(q, k_all, v_all, mask, lam)


def pool_mix(u_ext, n_hist, w_pool, scale):
    n, t_ext = u_ext.shape[0], u_ext.shape[1]
    uf = u_ext.astype(jnp.float32)
    cz = jnp.concatenate([jnp.zeros((n, 1, POOL_WIDTH), jnp.float32), jnp.cumsum(uf, axis=1)], axis=1)
    rows = np.arange(n_hist, t_ext)
    outs = []
    for g, win in enumerate(POOL_WINDOWS):
        c0, c1 = g * POOL_GROUP, (g + 1) * POOL_GROUP
        lo = np.maximum(rows + 1 - win, 0)
        cnt = (rows + 1 - lo).astype(np.float32)[None, :, None]
        m = (cz[:, rows + 1, c0:c1] - cz[:, lo, c0:c1]) / cnt - uf[:, n_hist:, c0:c1]
        outs.append(jnp.einsum('ntc,ce->nte', m.astype(u_ext.dtype), w_pool[g]))
    return jnp.concatenate(outs, axis=-1) * scale


def swiglu(h, w_in_l, w_out_l):
    gu = jnp.einsum('ntd,df->ntf', h, w_in_l)
    gate, up = jnp.split(gu, 2, axis=-1)
    return jnp.einsum('ntf,fd->ntd', jax.nn.silu(gate) * up, w_out_l)


def layer_tail(h, attn, pooled, w_out_l, w_ffn_in_l, w_ffn_out_l, ln_g_l, ln_b_l):
    mix = jnp.einsum('ntc,cd->ntd', jnp.concatenate([attn, pooled], axis=-1), w_out_l)
    h = layer_norm(DEEPNORM_ALPHA * h + mix, ln_g_l[0], ln_b_l[0])
    return layer_norm(DEEPNORM_ALPHA * h + swiglu(h, w_ffn_in_l, w_ffn_out_l), ln_g_l[1], ln_b_l[1])


def setup_inputs(seed: int = 0) -> dict:
    key = jax.random.key(seed)
    ks = jax.random.split(key, 18)
    n_pages = PAST_LEN // PAGE_SIZE
    n_used = DEC_BATCH * n_pages
    n_phys = n_used + n_used // 4
    perm = jax.random.permutation(ks[0], n_phys)
    page_table = perm[:n_used].reshape(DEC_BATCH, n_pages).astype(jnp.int32)
    f32 = jnp.float32
    nrm = lambda k, shape: jax.random.normal(k, shape, f32)
    return {
        'x_prompt': nrm(ks[1], (BATCH, SEQ, D_MODEL)),
        'x_sample': nrm(ks[2], (DEC_BATCH, DEC_SEQ, D_MODEL)),
        'cache_k': nrm(ks[3], (DEPTH, n_phys, PAGE_SIZE, N_HEADS, 2 * HEAD_DIM)),
        'cache_v': nrm(ks[4], (DEPTH, n_phys, PAGE_SIZE, N_HEADS, 2 * HEAD_DIM)),
        'state_pool': nrm(ks[5], (DEPTH, DEC_BATCH, POOL_HIST, POOL_WIDTH)),
        'page_table': page_table,
        'meta_tokens': nrm(ks[6], (N_META, D_MODEL)),
        'w_in': nrm(ks[7], (DEPTH, D_MODEL, PROJ_WIDTH)) * D_MODEL ** -0.5,
        'w_out': nrm(ks[8], (DEPTH, ATTN_WIDTH + POOL_WIDTH, D_MODEL)) * (D_MODEL ** -0.5 * DEEPNORM_BETA),
        'lambda_qk': nrm(ks[9], (DEPTH, 4, HEAD_DIM)) * 0.1,
        'subln_g': 1.0 + 0.02 * nrm(ks[10], (DEPTH, V_DIM)),
        'pool_w': nrm(ks[11], (DEPTH, N_POOL_GROUPS, POOL_GROUP, POOL_GROUP)) * POOL_GROUP ** -0.5,
        'pool_scale': 1.0 + 0.02 * nrm(ks[12], (DEPTH, POOL_WIDTH)),
        'w_ffn_in': nrm(ks[13], (DEPTH, D_MODEL, 2 * D_FF)) * D_MODEL ** -0.5,
        'w_ffn_out': nrm(ks[14], (DEPTH, D_FF, D_MODEL)) * (D_FF ** -0.5 * DEEPNORM_BETA),
        'ln_g': 1.0 + 0.02 * nrm(ks[15], (DEPTH, 2, D_MODEL)),
        'ln_b': 0.02 * nrm(ks[16], (DEPTH, 2, D_MODEL)),
    }


def reference(x_prompt, x_sample, cache_k, cache_v, state_pool, page_table, meta_tokens,
              w_in, w_out, lambda_qk, subln_g, pool_w, pool_scale, w_ffn_in, w_ffn_out,
              ln_g, ln_b):
    dt = x_prompt.dtype
    nb = x_prompt.shape[0]
    meta = jnp.broadcast_to(meta_tokens.astype(dt)[None], (nb, N_META, D_MODEL))
    hp = jnp.concatenate([meta, x_prompt], axis=1)
    hs = x_sample
    k_p, v_p, u_p, k_s, v_s, u_s = [], [], [], [], [], []
    for l in range(DEPTH):
        lam_init = 0.8 - 0.6 * math.exp(-0.3 * l)
        lam = diff_lambda(lambda_qk[l], lam_init)

        q, k, v, u = split_proj(hp, w_in[l])
        attn = head_norm(prompt_attention(q, k, v, lam), subln_g[l], lam_init)
        pooled = pool_mix(u, 0, pool_w[l], pool_scale[l])
        k_p.append(k.reshape(k.shape[0], k.shape[1], N_HEADS, 2 * HEAD_DIM))
        v_p.append(v)
        u_p.append(u[:, -POOL_HIST:])
        hp = layer_tail(hp, attn, pooled, w_out[l], w_ffn_in[l], w_ffn_out[l], ln_g[l], ln_b[l])

        q, k, v, u = split_proj(hs, w_in[l])
        attn = head_norm(sample_attention(q, k, v, cache_k[l], cache_v[l], page_table, lam),
                         subln_g[l], lam_init)
        u_ext = jnp.concatenate([state_pool[l], u], axis=1)
        pooled = pool_mix(u_ext, POOL_HIST, pool_w[l], pool_scale[l])
        k_s.append(k.reshape(k.shape[0], k.shape[1], N_HEADS, 2 * HEAD_DIM))
        v_s.append(v)
        u_s.append(u_ext[:, -POOL_HIST:])
        hs = layer_tail(hs, attn, pooled, w_out[l], w_ffn_in[l], w_ffn_out[l], ln_g[l], ln_b[l])

    y_prompt = hp[:, N_META:]
    y_sample = hs
    k_prompt = jnp.stack(k_p)
    v_prompt = jnp.stack(v_p)
    pool_prompt = jnp.stack(u_p)
    k_sample = jnp.stack(k_s)
    v_sample = jnp.stack(v_s)
    pool_sample = jnp.stack(u_s)
    return (y_prompt, y_sample, k_prompt, v_prompt, pool_prompt, k_sample, v_sample, pool_sample)
```

```python
import functools
import math

import jax
import jax.numpy as jnp
from jax import lax
from jax.experimental import pallas as pl
from jax.experimental.pallas import tpu as pltpu

D_MODEL = 1024
BATCH = 2
SEQ = 8192
DEPTH = 4
DEC_BATCH = 128
DEC_SEQ = 4
PAST_LEN = 2048
PAGE_SIZE = 128
N_META = 16
ATTN_WIDTH = 512
POOL_WIDTH = 512
HEAD_DIM = 64
V_DIM = 128
N_HEADS = 4
QK_WIDTH = 512
POOL_WINDOWS = (2, 4, 8, 16)
POOL_GROUP = 128
POOL_HIST = 15
PROJ_WIDTH = 2048
D_FF = 2816
DEEPNORM_ALPHA = (2 * DEPTH) ** 0.25
LN_EPS = 1e-5
SUBLN_EPS = 1e-5
QK_SCALE = HEAD_DIM ** -0.5

L_PROMPT = N_META + SEQ
TILE = 256
N_TILES = -(-L_PROMPT // TILE)
L_PAD = N_TILES * TILE
N_PAGES = PAST_LEN // PAGE_SIZE
N_SAMPLE_ROWS = DEC_BATCH * DEC_SEQ
HIST_PAD = 16
NEW_PAD = 16
VMEM_LIMIT_BYTES = 56 * 1024 * 1024

F32 = jnp.float32
BF16 = jnp.bfloat16
NT_DIMS = (((1,), (1,)), ((), ()))
TN_DIMS = (((0,), (0,)), ((), ()))


def _lam_init(layer):
    return 0.8 - 0.6 * math.exp(-0.3 * layer)


def _diff_lambda(lq, lam_init):
    a = jnp.sum(lq[0:1] * lq[1:2], axis=1, keepdims=True)
    b = jnp.sum(lq[2:3] * lq[3:4], axis=1, keepdims=True)
    return jnp.exp(a) - jnp.exp(b) + lam_init


def _layer_norm(x, g, b):
    mu = jnp.mean(x, axis=-1, keepdims=True)
    xc = x - mu
    var = jnp.mean(xc * xc, axis=-1, keepdims=True)
    return xc * lax.rsqrt(var + LN_EPS) * g + b


def _proj_prompt_kernel(h_ref, w_ref, wpool_ref, pscale_ref,
                        kf_ref, vf_ref, kb_ref, vt_ref, q1_ref, q2_ref,
                        pooled_ref, hist_ref, ubuf):
    j = pl.program_id(1)
    p = jnp.dot(h_ref[0].astype(BF16), w_ref[...], preferred_element_type=F32)
    k = p[:, QK_WIDTH:2 * QK_WIDTH]
    v = p[:, 2 * QK_WIDTH:2 * QK_WIDTH + ATTN_WIDTH]
    u = p[:, 2 * QK_WIDTH + ATTN_WIDTH:]
    kf_ref[0] = k
    vf_ref[0] = v
    kb_ref[0] = k.astype(BF16)
    vt_ref[0, 0] = v.T.astype(BF16)
    qt = (p[:, :QK_WIDTH] * QK_SCALE).T.astype(BF16)
    zeros = jnp.zeros((HEAD_DIM, TILE), BF16)
    for h in range(N_HEADS):
        r = h * 2 * HEAD_DIM
        q1_ref[0, r:r + HEAD_DIM, :] = qt[r:r + HEAD_DIM]
        q1_ref[0, r + HEAD_DIM:r + 2 * HEAD_DIM, :] = zeros
        q2_ref[0, r:r + HEAD_DIM, :] = zeros
        q2_ref[0, r + HEAD_DIM:r + 2 * HEAD_DIM, :] = qt[r + HEAD_DIM:r + 2 * HEAD_DIM]

    @pl.when(j == 0)
    def _():
        ubuf[0:HIST_PAD, :] = jnp.zeros((HIST_PAD, POOL_WIDTH), F32)

    @pl.when(j > 0)
    def _():
        ubuf[0:HIST_PAD, :] = ubuf[TILE:TILE + HIST_PAD, :]

    ubuf[HIST_PAD:HIST_PAD + TILE, :] = u
    pos = j * TILE + lax.broadcasted_iota(jnp.int32, (TILE, POOL_GROUP), 0)
    outs = []
    for g, win in enumerate(POOL_WINDOWS):
        c0 = g * POOL_GROUP
        s = ubuf[HIST_PAD:HIST_PAD + TILE, c0:c0 + POOL_GROUP]
        for i in range(1, win):
            s = s + ubuf[HIST_PAD - i:HIST_PAD - i + TILE, c0:c0 + POOL_GROUP]
        cnt = jnp.minimum(pos + 1, win).astype(F32)
        m = s / cnt - u[:, c0:c0 + POOL_GROUP]
        outs.append(jnp.dot(m.astype(BF16), wpool_ref[g], preferred_element_type=F32))
    pooled_ref[0] = (jnp.concatenate(outs, axis=-1) * pscale_ref[...]).astype(BF16)

    @pl.when(j == N_TILES - 1)
    def _():
        r0 = HIST_PAD + (L_PROMPT - POOL_HIST) - (N_TILES - 1) * TILE
        hist_ref[0] = ubuf[r0:r0 + POOL_HIST, :]


def _proj_prompt(hp, w_in_l, wpool_l, pscale_l):
    grid = (BATCH, N_TILES)
    row = lambda b, j: (b, j, 0)
    const2 = lambda b, j: (0, 0)
    const3 = lambda b, j: (0, 0, 0)
    out_shape = (
        jax.ShapeDtypeStruct((BATCH, L_PROMPT, QK_WIDTH), F32),
        jax.ShapeDtypeStruct((BATCH, L_PROMPT, ATTN_WIDTH), F32),
        jax.ShapeDtypeStruct((BATCH, L_PAD, QK_WIDTH), BF16),
        jax.ShapeDtypeStruct((BATCH, N_TILES, ATTN_WIDTH, TILE), BF16),
        jax.ShapeDtypeStruct((BATCH, QK_WIDTH, L_PAD), BF16),
        jax.ShapeDtypeStruct((BATCH, QK_WIDTH, L_PAD), BF16),
        jax.ShapeDtypeStruct((BATCH, L_PAD, POOL_WIDTH), BF16),
        jax.ShapeDtypeStruct((BATCH, POOL_HIST, POOL_WIDTH), F32),
    )
    out_specs = (
        pl.BlockSpec((1, TILE, QK_WIDTH), row),
        pl.BlockSpec((1, TILE, ATTN_WIDTH), row),
        pl.BlockSpec((1, TILE, QK_WIDTH), row),
        pl.BlockSpec((1, 1, ATTN_WIDTH, TILE), lambda b, j: (b, j, 0, 0)),
        pl.BlockSpec((1, QK_WIDTH, TILE), lambda b, j: (b, 0, j)),
        pl.BlockSpec((1, QK_WIDTH, TILE), lambda b, j: (b, 0, j)),
        pl.BlockSpec((1, TILE, POOL_WIDTH), row),
        pl.BlockSpec((1, POOL_HIST, POOL_WIDTH), lambda b, j: (b, 0, 0)),
    )
    in_specs = [
        pl.BlockSpec((1, TILE, D_MODEL), row),
        pl.BlockSpec((D_MODEL, PROJ_WIDTH), const2),
        pl.BlockSpec((len(POOL_WINDOWS), POOL_GROUP, POOL_GROUP), const3),
        pl.BlockSpec((1, POOL_WIDTH), const2),
    ]
    return pl.pallas_call(
        _proj_prompt_kernel,
        out_shape=out_shape,
        grid=grid,
        in_specs=in_specs,
        out_specs=out_specs,
        scratch_shapes=[pltpu.VMEM((HIST_PAD + TILE, POOL_WIDTH), F32)],
        compiler_params=pltpu.CompilerParams(
            dimension_semantics=("arbitrary", "arbitrary"),
            vmem_limit_bytes=VMEM_LIMIT_BYTES),
        name="proj_prompt",
    )(hp, w_in_l, wpool_l, pscale_l)


def _prompt_attn_kernel(lq_ref, g_ref, q1_ref, q2_ref, k_ref, vt_ref, o_ref,
                        m_sc, l_sc, acc_sc, *, lam_init):
    qi = pl.program_id(2)
    qs = (q1_ref[0], q2_ref[0])
    m_sc[...] = jnp.full(m_sc.shape, -jnp.inf, F32)
    l_sc[...] = jnp.zeros(l_sc.shape, F32)
    acc_sc[...] = jnp.zeros(acc_sc.shape, F32)

    def tile(j, masked):
        k = k_ref[0, pl.ds(pl.multiple_of(j * TILE, TILE), TILE), :]
        vt = vt_ref[0, j]
        for c in range(2):
            s = jnp.dot(k, qs[c], preferred_element_type=F32)
            if masked:
                kk = lax.broadcasted_iota(jnp.int32, (TILE, TILE), 0)
                qq = lax.broadcasted_iota(jnp.int32, (TILE, TILE), 1)
                s = jnp.where(kk <= qq, s, -jnp.inf)
            m_old = m_sc[c]
            m_new = jnp.maximum(m_old, jnp.max(s, axis=0, keepdims=True))
            a = jnp.exp(m_old - m_new)
            p = jnp.exp(s - m_new)
            l_sc[c] = a * l_sc[c] + jnp.sum(p, axis=0, keepdims=True)
            acc_sc[c] = a * acc_sc[c] + jnp.dot(vt, p.astype(BF16), preferred_element_type=F32)
            m_sc[c] = m_new

    def body(j, carry):
        tile(j, False)
        return carry

    lax.fori_loop(0, qi, body, 0)
    tile(qi, True)

    lam = _diff_lambda(lq_ref[...], lam_init)
    o = acc_sc[0] * (1.0 / l_sc[0]) - lam * (acc_sc[1] * (1.0 / l_sc[1]))
    ms = jnp.mean(o * o, axis=0, keepdims=True)
    o = o * lax.rsqrt(ms + SUBLN_EPS) * g_ref[...] * (1.0 - lam_init)
    o_ref[0] = o.T.astype(BF16)


def _prompt_attn(lq_l, g_col, q1, q2, kb, vt, layer):
    grid = (BATCH, N_HEADS, N_TILES)
    kern = functools.partial(_prompt_attn_kernel, lam_init=_lam_init(layer))
    return pl.pallas_call(
        kern,
        out_shape=jax.ShapeDtypeStruct((BATCH, L_PAD, ATTN_WIDTH), BF16),
        grid=grid,
        in_specs=[
            pl.BlockSpec((4, HEAD_DIM), lambda b, h, i: (0, 0)),
            pl.BlockSpec((V_DIM, 1), lambda b, h, i: (0, 0)),
            pl.BlockSpec((1, 2 * HEAD_DIM, TILE), lambda b, h, i: (b, h, i)),
            pl.BlockSpec((1, 2 * HEAD_DIM, TILE), lambda b, h, i: (b, h, i)),
            pl.BlockSpec((1, L_PAD, 2 * HEAD_DIM), lambda b, h, i: (b, 0, h)),
            pl.BlockSpec((1, N_TILES, V_DIM, TILE), lambda b, h, i: (b, 0, h, 0)),
        ],
        out_specs=pl.BlockSpec((1, TILE, V_DIM), lambda b, h, i: (b, i, h)),
        scratch_shapes=[
            pltpu.VMEM((2, 1, TILE), F32),
            pltpu.VMEM((2, 1, TILE), F32),
            pltpu.VMEM((2, V_DIM, TILE), F32),
        ],
        compiler_params=pltpu.CompilerParams(
            dimension_semantics=("arbitrary", "arbitrary", "arbitrary"),
            vmem_limit_bytes=VMEM_LIMIT_BYTES),
        name="prompt_attn",
    )(lq_l, g_col, q1, q2, kb, vt)


def _tail_kernel(h_ref, a_ref, p_ref, wo_ref, wfi_ref, wfo_ref, g_ref, b_ref, o_ref):
    h = h_ref[...]
    x = jnp.concatenate([a_ref[...], p_ref[...]], axis=-1)
    mix = jnp.dot(x, wo_ref[...], preferred_element_type=F32)
    h1 = _layer_norm(DEEPNORM_ALPHA * h + mix, g_ref[0:1], b_ref[0:1])
    gu = jnp.dot(h1.astype(BF16), wfi_ref[...], preferred_element_type=F32)
    gate = gu[:, :D_FF]
    act = gate * jax.nn.sigmoid(gate) * gu[:, D_FF:]
    y = jnp.dot(act.astype(BF16), wfo_ref[...], preferred_element_type=F32)
    o_ref[...] = _layer_norm(DEEPNORM_ALPHA * h1 + y, g_ref[1:2], b_ref[1:2])


def _tail(h, attn, pooled, wo_l, wfi_l, wfo_l, g_l, b_l):
    n_rows = h.shape[0]
    grid = (n_rows // TILE,)
    row = lambda i: (i, 0)
    const = lambda i: (0, 0)
    once = pl.Buffered(1)
    return pl.pallas_call(
        _tail_kernel,
        out_shape=jax.ShapeDtypeStruct((n_rows, D_MODEL), F32),
        grid=grid,
        in_specs=[
            pl.BlockSpec((TILE, D_MODEL), row),
            pl.BlockSpec((TILE, ATTN_WIDTH), row),
            pl.BlockSpec((TILE, POOL_WIDTH), row),
            pl.BlockSpec((D_MODEL, D_MODEL), const, pipeline_mode=once),
            pl.BlockSpec((D_MODEL, 2 * D_FF), const, pipeline_mode=once),
            pl.BlockSpec((D_FF, D_MODEL), const, pipeline_mode=once),
            pl.BlockSpec((2, D_MODEL), const),
            pl.BlockSpec((2, D_MODEL), const),
        ],
        out_specs=pl.BlockSpec((TILE, D_MODEL), row),
        compiler_params=pltpu.CompilerParams(
            dimension_semantics=("arbitrary",),
            vmem_limit_bytes=VMEM_LIMIT_BYTES),
        name="layer_tail",
    )(h, attn, pooled, wo_l, wfi_l, wfo_l, g_l, b_l)


def _proj_sample_kernel(h_ref, w_ref, wpool_ref, pscale_ref, sp_ref,
                        kf_ref, vf_ref, qx_ref, knew_ref, vnew_ref, pooled_ref, hist_ref):
    p = jnp.dot(h_ref[...].astype(BF16), w_ref[...], preferred_element_type=F32)
    q = p[:, :QK_WIDTH] * QK_SCALE
    k = p[:, QK_WIDTH:2 * QK_WIDTH]
    v = p[:, 2 * QK_WIDTH:2 * QK_WIDTH + ATTN_WIDTH]
    u = p[:, 2 * QK_WIDTH + ATTN_WIDTH:]
    kf_ref[...] = k
    vf_ref[...] = v
    lane = lax.broadcasted_iota(jnp.int32, (DEC_BATCH, QK_WIDTH), 1)
    for t in range(DEC_SEQ):
        rows = slice(t * DEC_BATCH, (t + 1) * DEC_BATCH)
        qt = q[rows]
        for c in range(2):
            for h in range(N_HEADS):
                lo = h * 2 * HEAD_DIM + c * HEAD_DIM
                keep = (lane >= lo) & (lane < lo + HEAD_DIM)
                qx_ref[c * N_HEADS * DEC_SEQ + h * DEC_SEQ + t] = jnp.where(keep, qt, 0.0).astype(BF16)
        knew_ref[t] = k[rows].astype(BF16)
        vnew_ref[t] = v[rows].astype(BF16)
    pad = jnp.zeros((NEW_PAD - DEC_SEQ, DEC_BATCH, QK_WIDTH), BF16)
    knew_ref[DEC_SEQ:NEW_PAD] = pad
    vnew_ref[DEC_SEQ:NEW_PAD] = pad

    ext = [sp_ref[:, i * POOL_WIDTH:(i + 1) * POOL_WIDTH] for i in range(POOL_HIST)]
    ext += [u[t * DEC_BATCH:(t + 1) * DEC_BATCH] for t in range(DEC_SEQ)]
    for t in range(DEC_SEQ):
        r = POOL_HIST + t
        outs = []
        for g, win in enumerate(POOL_WINDOWS):
            cols = slice(g * POOL_GROUP, (g + 1) * POOL_GROUP)
            s = ext[r][:, cols]
            for i in range(1, win):
                s = s + ext[r - i][:, cols]
            m = s / float(win) - ext[r][:, cols]
            outs.append(jnp.dot(m.astype(BF16), wpool_ref[g], preferred_element_type=F32))
        pooled_ref[t * DEC_BATCH:(t + 1) * DEC_BATCH, :] = (
            jnp.concatenate(outs, axis=-1) * pscale_ref[...]).astype(BF16)
    keep_rows = POOL_HIST - DEC_SEQ
    hist_ref[:, 0:keep_rows * POOL_WIDTH] = sp_ref[:, DEC_SEQ * POOL_WIDTH:POOL_HIST * POOL_WIDTH]
    for t in range(DEC_SEQ):
        hist_ref[:, (keep_rows + t) * POOL_WIDTH:(keep_rows + t + 1) * POOL_WIDTH] = ext[POOL_HIST + t]


def _proj_sample(hs, w_in_l, wpool_l, pscale_l, sp_l):
    n_qx = 2 * N_HEADS * DEC_SEQ
    out_shape = (
        jax.ShapeDtypeStruct((N_SAMPLE_ROWS, QK_WIDTH), F32),
        jax.ShapeDtypeStruct((N_SAMPLE_ROWS, ATTN_WIDTH), F32),
        jax.ShapeDtypeStruct((n_qx, DEC_BATCH, QK_WIDTH), BF16),
        jax.ShapeDtypeStruct((NEW_PAD, DEC_BATCH, QK_WIDTH), BF16),
        jax.ShapeDtypeStruct((NEW_PAD, DEC_BATCH, ATTN_WIDTH), BF16),
        jax.ShapeDtypeStruct((N_SAMPLE_ROWS, POOL_WIDTH), BF16),
        jax.ShapeDtypeStruct((DEC_BATCH, POOL_HIST * POOL_WIDTH), F32),
    )
    return pl.pallas_call(
        _proj_sample_kernel,
        out_shape=out_shape,
        compiler_params=pltpu.CompilerParams(vmem_limit_bytes=VMEM_LIMIT_BYTES),
        name="proj_sample",
    )(hs, w_in_l, wpool_l, pscale_l, sp_l)


def _sample_attn_kernel(pt_ref, lq_ref, g_ref, qx_ref, knew_ref, vnew_ref, ck_hbm, cv_hbm,
                        o_ref, kbuf, vbuf, sem, *, layer, lam_init):
    b = pl.program_id(0)
    n_b = pl.num_programs(0)
    slot = b % 2

    def page_copies(seq, sl):
        cps = []
        for i in range(N_PAGES):
            page = pt_ref[seq, i]
            rows = pl.ds(i * PAGE_SIZE, PAGE_SIZE)
            cps.append(pltpu.make_async_copy(ck_hbm.at[layer, page], kbuf.at[sl, rows, :], sem.at[0, sl]))
            cps.append(pltpu.make_async_copy(cv_hbm.at[layer, page], vbuf.at[sl, rows, :], sem.at[1, sl]))
        return cps

    @pl.when(b == 0)
    def _():
        for cp in page_copies(0, 0):
            cp.start()

    @pl.when(b + 1 < n_b)
    def _():
        for cp in page_copies(b + 1, 1 - slot):
            cp.start()

    for cp in page_copies(b, slot):
        cp.wait()

    qx = qx_ref[...]
    n_col = qx.shape[0]
    s_past = lax.dot_general(kbuf[slot].astype(BF16), qx, NT_DIMS,
                             preferred_element_type=F32)
    s_new = lax.dot_general(knew_ref[...], qx, NT_DIMS, preferred_element_type=F32)
    jj = lax.broadcasted_iota(jnp.int32, (NEW_PAD, n_col), 0)
    tt = lax.broadcasted_iota(jnp.int32, (NEW_PAD, n_col), 1) % DEC_SEQ
    s_new = jnp.where(jj <= tt, s_new, -jnp.inf)
    m = jnp.maximum(jnp.max(s_past, axis=0, keepdims=True), jnp.max(s_new, axis=0, keepdims=True))
    p_past = jnp.exp(s_past - m)
    p_new = jnp.exp(s_new - m)
    inv = 1.0 / (jnp.sum(p_past, axis=0, keepdims=True) + jnp.sum(p_new, axis=0, keepdims=True))
    r = lax.dot_general((p_past * inv).astype(BF16), vbuf[slot].astype(BF16), TN_DIMS,
                        preferred_element_type=F32)
    r = r + lax.dot_general((p_new * inv).astype(BF16), vnew_ref[...], TN_DIMS,
                            preferred_element_type=F32)
    lam = _diff_lambda(lq_ref[...], lam_init)
    half = n_col // 2
    o = r[0:half] - lam * r[half:n_col]
    lane_h = lax.broadcasted_iota(jnp.int32, (half, ATTN_WIDTH), 1) // V_DIM
    row_h = lax.broadcasted_iota(jnp.int32, (half, ATTN_WIDTH), 0) // DEC_SEQ
    o = jnp.where(lane_h == row_h, o, 0.0)
    ms = jnp.sum(o * o, axis=1, keepdims=True) * (1.0 / V_DIM)
    o = o * lax.rsqrt(ms + SUBLN_EPS) * g_ref[...] * (1.0 - lam_init)
    out = o[0:DEC_SEQ]
    for h in range(1, N_HEADS):
        out = out + o[h * DEC_SEQ:(h + 1) * DEC_SEQ]
    o_ref[...] = out.astype(BF16)


def _sample_attn(page_table, lq_l, g_row, qx, knew, vnew, cache_k, cache_v, layer):
    n_qx = qx.shape[1]
    kern = functools.partial(_sample_attn_kernel, layer=layer, lam_init=_lam_init(layer))
    grid_spec = pltpu.PrefetchScalarGridSpec(
        num_scalar_prefetch=1,
        grid=(DEC_BATCH,),
        in_specs=[
            pl.BlockSpec((4, HEAD_DIM), lambda b, pt: (0, 0)),
            pl.BlockSpec((1, ATTN_WIDTH), lambda b, pt: (0, 0)),
            pl.BlockSpec((None, n_qx, QK_WIDTH), lambda b, pt: (b, 0, 0)),
            pl.BlockSpec((None, NEW_PAD, QK_WIDTH), lambda b, pt: (b, 0, 0)),
            pl.BlockSpec((None, NEW_PAD, ATTN_WIDTH), lambda b, pt: (b, 0, 0)),
            pl.BlockSpec(memory_space=pl.ANY),
            pl.BlockSpec(memory_space=pl.ANY),
        ],
        out_specs=pl.BlockSpec((None, DEC_SEQ, ATTN_WIDTH), lambda b, pt: (b, 0, 0)),
        scratch_shapes=[
            pltpu.VMEM((2, PAST_LEN, QK_WIDTH), F32),
            pltpu.VMEM((2, PAST_LEN, ATTN_WIDTH), F32),
            pltpu.SemaphoreType.DMA((2, 2)),
        ],
    )
    return pl.pallas_call(
        kern,
        out_shape=jax.ShapeDtypeStruct((DEC_BATCH, DEC_SEQ, ATTN_WIDTH), BF16),
        grid_spec=grid_spec,
        compiler_params=pltpu.CompilerParams(
            dimension_semantics=("arbitrary",),
            vmem_limit_bytes=VMEM_LIMIT_BYTES),
        name="sample_attn",
    )(page_table, lq_l, g_row, qx, knew, vnew, cache_k, cache_v)


def kernel(x_prompt, x_sample, cache_k, cache_v, state_pool, page_table, meta_tokens,
           w_in, w_out, lambda_qk, subln_g, pool_w, pool_scale, w_ffn_in, w_ffn_out,
           ln_g, ln_b):
    assert x_prompt.shape == (BATCH, SEQ, D_MODEL)
    assert x_sample.shape == (DEC_BATCH, DEC_SEQ, D_MODEL)
    n_phys = cache_k.shape[1]
    ck = cache_k.reshape(DEPTH, n_phys, PAGE_SIZE, N_HEADS * 2 * HEAD_DIM)
    cv = cache_v.reshape(DEPTH, n_phys, PAGE_SIZE, N_HEADS * V_DIM)
    sp = state_pool.reshape(DEPTH, DEC_BATCH, POOL_HIST * POOL_WIDTH)

    w_in_b = w_in.astype(BF16)
    w_out_b = w_out.astype(BF16)
    w_ffn_in_b = w_ffn_in.astype(BF16)
    w_ffn_out_b = w_ffn_out.astype(BF16)
    pool_w_b = pool_w.astype(BF16)

    meta = jnp.broadcast_to(meta_tokens[None], (BATCH, N_META, D_MODEL))
    tail_pad = jnp.zeros((BATCH, L_PAD - L_PROMPT, D_MODEL), F32)
    hp = jnp.concatenate([meta, x_prompt, tail_pad], axis=1)
    hs = jnp.transpose(x_sample, (1, 0, 2)).reshape(N_SAMPLE_ROWS, D_MODEL)

    k_p, v_p, u_p, k_s, v_s, u_s = [], [], [], [], [], []
    for l in range(DEPTH):
        pscale = pool_scale[l].reshape(1, POOL_WIDTH)
        g_col = subln_g[l].reshape(V_DIM, 1)
        g_row = jnp.tile(subln_g[l], N_HEADS).reshape(1, ATTN_WIDTH)

        kf, vf, kb, vt, q1, q2, pooled, hist = _proj_prompt(hp, w_in_b[l], pool_w_b[l], pscale)
        attn = _prompt_attn(lambda_qk[l], g_col, q1, q2, kb, vt, l)
        hp = _tail(hp.reshape(BATCH * L_PAD, D_MODEL),
                   attn.reshape(BATCH * L_PAD, ATTN_WIDTH),
                   pooled.reshape(BATCH * L_PAD, POOL_WIDTH),
                   w_out_b[l], w_ffn_in_b[l], w_ffn_out_b[l], ln_g[l], ln_b[l])
        hp = hp.reshape(BATCH, L_PAD, D_MODEL)
        k_p.append(kf)
        v_p.append(vf)
        u_p.append(hist)

        kfs, vfs, qx, knew, vnew, pooled_s, hist_s = _proj_sample(hs, w_in_b[l], pool_w_b[l], pscale, sp[l])
        to_seq_major = lambda x: jnp.transpose(x, (1, 0, 2))
        attn_s = _sample_attn(page_table, lambda_qk[l], g_row, to_seq_major(qx), to_seq_major(knew),
                              to_seq_major(vnew), ck, cv, l)
        hs = _tail(hs, to_seq_major(attn_s).reshape(N_SAMPLE_ROWS, ATTN_WIDTH), pooled_s,
                   w_out_b[l], w_ffn_in_b[l], w_ffn_out_b[l], ln_g[l], ln_b[l])
        k_s.append(kfs)
        v_s.append(vfs)
        u_s.append(hist_s)

    def prompt_heads(xs):
        return jnp.stack(xs).reshape(DEPTH, BATCH, L_PROMPT, N_HEADS, 2 * HEAD_DIM)

    def sample_heads(xs):
        x = jnp.stack(xs).reshape(DEPTH, DEC_SEQ, DEC_BATCH, N_HEADS, 2 * HEAD_DIM)
        return jnp.transpose(x, (0, 2, 1, 3, 4))

    y_prompt = hp[:, N_META:L_PROMPT]
    y_sample = jnp.transpose(hs.reshape(DEC_SEQ, DEC_BATCH, D_MODEL), (1, 0, 2))
    return (y_prompt, y_sample, prompt_heads(k_p), prompt_heads(v_p), jnp.stack(u_p),
            sample_heads(k_s), sample_heads(v_s),
            jnp.stack(u_s).reshape(DEPTH, DEC_BATCH, POOL_HIST, POOL_WIDTH))
```

```python
import functools
import math

import jax
import jax.numpy as jnp
from jax import lax
from jax.experimental import pallas as pl
from jax.experimental.pallas import tpu as pltpu

D_MODEL = 1024
BATCH = 2
SEQ = 8192
DEPTH = 4
DEC_BATCH = 128
DEC_SEQ = 4
PAST_LEN = 2048
PAGE_SIZE = 128
N_META = 16
ATTN_WIDTH = 512
POOL_WIDTH = 512
HEAD_DIM = 64
V_DIM = 128
N_HEADS = 4
QK_WIDTH = 512
POOL_WINDOWS = (2, 4, 8, 16)
POOL_GROUP = 128
POOL_HIST = 15
PROJ_WIDTH = 2048
D_FF = 2816
DEEPNORM_ALPHA = (2 * DEPTH) ** 0.25
LN_EPS = 1e-5
SUBLN_EPS = 1e-5
QK_SCALE = HEAD_DIM ** -0.5
LOG2_E = math.log2(math.e)

L_PROMPT = N_META + SEQ
TILE = 512
N_TILES = -(-L_PROMPT // TILE)
L_PAD = N_TILES * TILE
LAST_Q = L_PROMPT - (N_TILES - 1) * TILE
LAST_W = -(-LAST_Q // 128) * 128
V_EXT = V_DIM + 16
N_PAGES = PAST_LEN // PAGE_SIZE
PAGE_ROWS = PAGE_SIZE * N_HEADS
N_SAMPLE_ROWS = DEC_BATCH * DEC_SEQ
HIST_PAD = 16
NEW_PAD = 16
VMEM_LIMIT_BYTES = 56 * 1024 * 1024

F32 = jnp.float32
BF16 = jnp.bfloat16
NT_DIMS = (((1,), (1,)), ((), ()))


def _lam_init(layer):
    return 0.8 - 0.6 * math.exp(-0.3 * layer)


def _diff_lambda(lq, lam_init):
    a = jnp.sum(lq[0:1] * lq[1:2], axis=1, keepdims=True)
    b = jnp.sum(lq[2:3] * lq[3:4], axis=1, keepdims=True)
    return jnp.exp(a) - jnp.exp(b) + lam_init


def _layer_norm(x, g, b):
    mu = jnp.mean(x, axis=-1, keepdims=True)
    xc = x - mu
    var = jnp.mean(xc * xc, axis=-1, keepdims=True)
    return xc * lax.rsqrt(var + LN_EPS) * g + b


def _proj_prompt_kernel(h_ref, w_ref, wpool_ref, pscale_ref, k_all_ref, v_all_ref,
                        kf_ref, vf_ref, kb_ref, vt_ref, qx_ref, pooled_ref, hist_ref, ubuf):
    del k_all_ref, v_all_ref
    j = pl.program_id(1)
    p = jnp.dot(h_ref[0].astype(BF16), w_ref[...], preferred_element_type=F32)
    k = p[:, QK_WIDTH:2 * QK_WIDTH]
    v = p[:, 2 * QK_WIDTH:2 * QK_WIDTH + ATTN_WIDTH]
    u = p[:, 2 * QK_WIDTH + ATTN_WIDTH:]
    for h in range(N_HEADS):
        cols = slice(h * V_DIM, (h + 1) * V_DIM)
        kf_ref[0, 0, pl.ds(h, TILE, stride=N_HEADS), :] = k[:, cols]
        vf_ref[0, 0, pl.ds(h, TILE, stride=N_HEADS), :] = v[:, cols]
    kb_ref[0] = k.astype(BF16)
    vt = v.T.astype(BF16)
    ones = jnp.ones((V_EXT - V_DIM, TILE), BF16)
    for h in range(N_HEADS):
        vt_ref[0, 0, h * V_EXT:h * V_EXT + V_DIM, :] = vt[h * V_DIM:(h + 1) * V_DIM]
        vt_ref[0, 0, h * V_EXT + V_DIM:(h + 1) * V_EXT, :] = ones
    qt = (p[:, :QK_WIDTH] * (QK_SCALE * LOG2_E)).T.astype(BF16)
    zeros = jnp.zeros((HEAD_DIM, TILE), BF16)
    for h in range(N_HEADS):
        r = h * 2 * HEAD_DIM
        qx_ref[0, 0, r:r + HEAD_DIM, 0:TILE] = qt[r:r + HEAD_DIM]
        qx_ref[0, 0, r + HEAD_DIM:r + 2 * HEAD_DIM, 0:TILE] = zeros
        qx_ref[0, 0, r:r + HEAD_DIM, TILE:2 * TILE] = zeros
        qx_ref[0, 0, r + HEAD_DIM:r + 2 * HEAD_DIM, TILE:2 * TILE] = qt[r + HEAD_DIM:r + 2 * HEAD_DIM]

    @pl.when(j == 0)
    def _():
        ubuf[0:HIST_PAD, :] = jnp.zeros((HIST_PAD, POOL_WIDTH), F32)

    @pl.when(j > 0)
    def _():
        ubuf[0:HIST_PAD, :] = ubuf[TILE:TILE + HIST_PAD, :]

    ubuf[HIST_PAD:HIST_PAD + TILE, :] = u
    pos = j * TILE + lax.broadcasted_iota(jnp.int32, (TILE, POOL_GROUP), 0)
    outs = []
    for g, win in enumerate(POOL_WINDOWS):
        c0 = g * POOL_GROUP
        s = ubuf[HIST_PAD:HIST_PAD + TILE, c0:c0 + POOL_GROUP]
        for i in range(1, win):
            s = s + ubuf[HIST_PAD - i:HIST_PAD - i + TILE, c0:c0 + POOL_GROUP]
        cnt = jnp.minimum(pos + 1, win).astype(F32)
        m = s / cnt - u[:, c0:c0 + POOL_GROUP]
        outs.append(jnp.dot(m.astype(BF16), wpool_ref[g], preferred_element_type=F32))
    pooled_ref[0] = (jnp.concatenate(outs, axis=-1) * pscale_ref[...]).astype(BF16)

    @pl.when(j == N_TILES - 1)
    def _():
        r0 = HIST_PAD + (L_PROMPT - POOL_HIST) - (N_TILES - 1) * TILE
        hist_ref[0] = ubuf[r0:r0 + POOL_HIST, :]


def _proj_prompt(hp, w_in_l, wpool_l, pscale_l, k_all, v_all, layer):
    grid = (BATCH, N_TILES)
    row = lambda b, j: (b, j, 0)
    const2 = lambda b, j: (0, 0)
    const3 = lambda b, j: (0, 0, 0)
    kv_shape = jax.ShapeDtypeStruct((DEPTH, BATCH, L_PROMPT * N_HEADS, V_DIM), F32)
    kv_spec = pl.BlockSpec((1, 1, TILE * N_HEADS, V_DIM), lambda b, j: (layer, b, j, 0))
    out_shape = (
        kv_shape,
        kv_shape,
        jax.ShapeDtypeStruct((BATCH, L_PAD, QK_WIDTH), BF16),
        jax.ShapeDtypeStruct((BATCH, N_TILES, N_HEADS * V_EXT, TILE), BF16),
        jax.ShapeDtypeStruct((BATCH, N_TILES, QK_WIDTH, 2 * TILE), BF16),
        jax.ShapeDtypeStruct((BATCH, L_PAD, POOL_WIDTH), BF16),
        jax.ShapeDtypeStruct((BATCH, POOL_HIST, POOL_WIDTH), F32),
    )
    out_specs = (
        kv_spec,
        kv_spec,
        pl.BlockSpec((1, TILE, QK_WIDTH), row),
        pl.BlockSpec((1, 1, N_HEADS * V_EXT, TILE), lambda b, j: (b, j, 0, 0)),
        pl.BlockSpec((1, 1, QK_WIDTH, 2 * TILE), lambda b, j: (b, j, 0, 0)),
        pl.BlockSpec((1, TILE, POOL_WIDTH), row),
        pl.BlockSpec((1, POOL_HIST, POOL_WIDTH), lambda b, j: (b, 0, 0)),
    )
    in_specs = [
        pl.BlockSpec((1, TILE, D_MODEL), row),
        pl.BlockSpec((D_MODEL, PROJ_WIDTH), const2),
        pl.BlockSpec((len(POOL_WINDOWS), POOL_GROUP, POOL_GROUP), const3),
        pl.BlockSpec((1, POOL_WIDTH), const2),
        pl.BlockSpec(memory_space=pl.ANY),
        pl.BlockSpec(memory_space=pl.ANY),
    ]
    return pl.pallas_call(
        _proj_prompt_kernel,
        out_shape=out_shape,
        grid=grid,
        in_specs=in_specs,
        out_specs=out_specs,
        scratch_shapes=[pltpu.VMEM((HIST_PAD + TILE, POOL_WIDTH), F32)],
        input_output_aliases={4: 0, 5: 1},
        compiler_params=pltpu.CompilerParams(
            dimension_semantics=("arbitrary", "arbitrary"),
            vmem_limit_bytes=VMEM_LIMIT_BYTES),
        name="proj_prompt",
    )(hp, w_in_l, wpool_l, pscale_l, k_all, v_all)


def _prompt_attn_tile(width, lq_ref, g_ref, qx_ref, k_ref, vt_ref, o_ref,
                      s0_sc, s1_sc, mx0_sc, mx1_sc, m_sc, acc_sc, *, lam_init):
    qi = pl.program_id(2)
    slots = ((s0_sc, mx0_sc), (s1_sc, mx1_sc))
    lanes = slice(0, 2 * width)
    if width == TILE:
        q = qx_ref[0, 0]
    else:
        q = jnp.concatenate([qx_ref[0, 0, :, 0:width], qx_ref[0, 0, :, TILE:TILE + width]], axis=1)
    m_sc[:, lanes] = jnp.full((1, 2 * width), -jnp.inf, F32)
    acc_sc[:, lanes] = jnp.zeros((V_EXT, 2 * width), F32)

    def scores(j, slot):
        s_sc, mx_sc = slots[slot]
        k = k_ref[0, pl.ds(pl.multiple_of(j * TILE, TILE), TILE), :]
        s = jnp.dot(k, q, preferred_element_type=F32)
        s_sc[:, lanes] = s
        mx_sc[:, lanes] = jnp.max(s, axis=0, keepdims=True)

    def softmax_pv(j, slot, masked):
        s_sc, mx_sc = slots[slot]
        s = s_sc[:, lanes]
        if masked:
            kk = lax.broadcasted_iota(jnp.int32, (TILE, 2 * width), 0)
            qq = lax.broadcasted_iota(jnp.int32, (TILE, 2 * width), 1) & (width - 1)
            s = jnp.where(kk <= qq, s, -jnp.inf)
            mx = jnp.max(s, axis=0, keepdims=True)
        else:
            mx = mx_sc[:, lanes]
        m_old = m_sc[:, lanes]
        m_new = jnp.maximum(m_old, mx)
        a = jnp.exp2(m_old - m_new)
        p = jnp.exp2(s - m_new).astype(BF16)
        acc_sc[:, lanes] = a * acc_sc[:, lanes] + jnp.dot(vt_ref[0, j], p, preferred_element_type=F32)
        m_sc[:, lanes] = m_new

    scores(0, 0)

    def pair(jj, carry):
        j = 2 * jj
        scores(j + 1, 1)
        softmax_pv(j, 0, False)
        scores(j + 2, 0)
        softmax_pv(j + 1, 1, False)
        return carry

    lax.fori_loop(0, qi // 2, pair, 0)

    @pl.when(qi % 2 == 0)
    def _():
        softmax_pv(qi, 0, True)

    @pl.when(qi % 2 == 1)
    def _():
        scores(qi, 1)
        softmax_pv(qi - 1, 0, False)
        softmax_pv(qi, 1, True)

    lam = _diff_lambda(lq_ref[...], lam_init)
    inv = 1.0 / acc_sc[V_DIM:V_DIM + 1, lanes]
    o = (acc_sc[0:V_DIM, 0:width] * inv[:, 0:width]
         - lam * (acc_sc[0:V_DIM, width:2 * width] * inv[:, width:2 * width]))
    ms = jnp.mean(o * o, axis=0, keepdims=True)
    o = o * lax.rsqrt(ms + SUBLN_EPS) * g_ref[...] * (1.0 - lam_init)
    o_ref[0, 0:width, :] = o.T.astype(BF16)
    if width < TILE:
        o_ref[0, width:TILE, :] = jnp.zeros((TILE - width, V_DIM), BF16)


def _prompt_attn_kernel(*refs, lam_init):
    qi = pl.program_id(2)

    @pl.when(qi < N_TILES - 1)
    def _():
        _prompt_attn_tile(TILE, *refs, lam_init=lam_init)

    @pl.when(qi == N_TILES - 1)
    def _():
        _prompt_attn_tile(LAST_W, *refs, lam_init=lam_init)


def _prompt_attn(lq_l, g_col, qx, kb, vt, layer):
    grid = (BATCH, N_HEADS, N_TILES)
    kern = functools.partial(_prompt_attn_kernel, lam_init=_lam_init(layer))
    return pl.pallas_call(
        kern,
        out_shape=jax.ShapeDtypeStruct((BATCH, L_PAD, ATTN_WIDTH), BF16),
        grid=grid,
        in_specs=[
            pl.BlockSpec((4, HEAD_DIM), lambda b, h, i: (0, 0)),
            pl.BlockSpec((V_DIM, 1), lambda b, h, i: (0, 0)),
            pl.BlockSpec((1, 1, 2 * HEAD_DIM, 2 * TILE), lambda b, h, i: (b, i, h, 0)),
            pl.BlockSpec((1, L_PAD, 2 * HEAD_DIM), lambda b, h, i: (b, 0, h)),
            pl.BlockSpec((1, N_TILES, V_EXT, TILE), lambda b, h, i: (b, 0, h, 0)),
        ],
        out_specs=pl.BlockSpec((1, TILE, V_DIM), lambda b, h, i: (b, i, h)),
        scratch_shapes=[
            pltpu.VMEM((TILE, 2 * TILE), F32),
            pltpu.VMEM((TILE, 2 * TILE), F32),
            pltpu.VMEM((1, 2 * TILE), F32),
            pltpu.VMEM((1, 2 * TILE), F32),
            pltpu.VMEM((1, 2 * TILE), F32),
            pltpu.VMEM((V_EXT, 2 * TILE), F32),
        ],
        compiler_params=pltpu.CompilerParams(
            dimension_semantics=("arbitrary", "arbitrary", "arbitrary"),
            vmem_limit_bytes=VMEM_LIMIT_BYTES),
        name="prompt_attn",
    )(lq_l, g_col, qx, kb, vt)


def _tail_kernel(h_ref, a_ref, p_ref, wo_ref, wfi_ref, wfo_ref, g_ref, b_ref, o_ref):
    h = h_ref[...]
    x = jnp.concatenate([a_ref[...], p_ref[...]], axis=-1)
    mix = jnp.dot(x, wo_ref[...], preferred_element_type=F32)
    h1 = _layer_norm(DEEPNORM_ALPHA * h + mix, g_ref[0:1], b_ref[0:1])
    gu = jnp.dot(h1.astype(BF16), wfi_ref[...], preferred_element_type=F32)
    gate = gu[:, :D_FF]
    act = gate * jax.nn.sigmoid(gate) * gu[:, D_FF:]
    y = jnp.dot(act.astype(BF16), wfo_ref[...], preferred_element_type=F32)
    o_ref[...] = _layer_norm(DEEPNORM_ALPHA * h1 + y, g_ref[1:2], b_ref[1:2])


def _tail(h, attn, pooled, wo_l, wfi_l, wfo_l, g_l, b_l):
    n_rows = h.shape[0]
    grid = (n_rows // TILE,)
    row = lambda i: (i, 0)
    const = lambda i: (0, 0)
    once = pl.Buffered(1)
    return pl.pallas_call(
        _tail_kernel,
        out_shape=jax.ShapeDtypeStruct((n_rows, D_MODEL), F32),
        grid=grid,
        in_specs=[
            pl.BlockSpec((TILE, D_MODEL), row),
            pl.BlockSpec((TILE, ATTN_WIDTH), row),
            pl.BlockSpec((TILE, POOL_WIDTH), row),
            pl.BlockSpec((D_MODEL, D_MODEL), const, pipeline_mode=once),
            pl.BlockSpec((D_MODEL, 2 * D_FF), const, pipeline_mode=once),
            pl.BlockSpec((D_FF, D_MODEL), const, pipeline_mode=once),
            pl.BlockSpec((2, D_MODEL), const),
            pl.BlockSpec((2, D_MODEL), const),
        ],
        out_specs=pl.BlockSpec((TILE, D_MODEL), row),
        compiler_params=pltpu.CompilerParams(
            dimension_semantics=("arbitrary",),
            vmem_limit_bytes=VMEM_LIMIT_BYTES),
        name="layer_tail",
    )(h, attn, pooled, wo_l, wfi_l, wfo_l, g_l, b_l)


def _proj_sample_kernel(h_ref, w_ref, wpool_ref, pscale_ref, sp_ref,
                        kf_ref, vf_ref, qx_ref, knew_ref, vnew_ref, pooled_ref, hist_ref):
    p = jnp.dot(h_ref[...].astype(BF16), w_ref[...], preferred_element_type=F32)
    q = p[:, :QK_WIDTH] * QK_SCALE
    k = p[:, QK_WIDTH:2 * QK_WIDTH]
    v = p[:, 2 * QK_WIDTH:2 * QK_WIDTH + ATTN_WIDTH]
    u = p[:, 2 * QK_WIDTH + ATTN_WIDTH:]
    kf_ref[...] = k
    vf_ref[...] = v
    lane = lax.broadcasted_iota(jnp.int32, (DEC_BATCH, QK_WIDTH), 1)
    for t in range(DEC_SEQ):
        rows = slice(t * DEC_BATCH, (t + 1) * DEC_BATCH)
        qt = q[rows]
        for c in range(2):
            for h in range(N_HEADS):
                lo = h * 2 * HEAD_DIM + c * HEAD_DIM
                keep = (lane >= lo) & (lane < lo + HEAD_DIM)
                qx_ref[c * N_HEADS * DEC_SEQ + h * DEC_SEQ + t] = jnp.where(keep, qt, 0.0).astype(BF16)
        knew_ref[t] = k[rows].astype(BF16)
        vnew_ref[t] = v[rows].astype(BF16)
    pad = jnp.zeros((NEW_PAD - DEC_SEQ, DEC_BATCH, QK_WIDTH), BF16)
    knew_ref[DEC_SEQ:NEW_PAD] = pad
    vnew_ref[DEC_SEQ:NEW_PAD] = pad

    ext = [sp_ref[:, i * POOL_WIDTH:(i + 1) * POOL_WIDTH] for i in range(POOL_HIST)]
    ext += [u[t * DEC_BATCH:(t + 1) * DEC_BATCH] for t in range(DEC_SEQ)]
    for t in range(DEC_SEQ):
        r = POOL_HIST + t
        outs = []
        for g, win in enumerate(POOL_WINDOWS):
            cols = slice(g * POOL_GROUP, (g + 1) * POOL_GROUP)
            s = ext[r][:, cols]
            for i in range(1, win):
                s = s + ext[r - i][:, cols]
            m = s / float(win) - ext[r][:, cols]
            outs.append(jnp.dot(m.astype(BF16), wpool_ref[g], preferred_element_type=F32))
        pooled_ref[t * DEC_BATCH:(t + 1) * DEC_BATCH, :] = (
            jnp.concatenate(outs, axis=-1) * pscale_ref[...]).astype(BF16)
    keep_rows = POOL_HIST - DEC_SEQ
    hist_ref[:, 0:keep_rows * POOL_WIDTH] = sp_ref[:, DEC_SEQ * POOL_WIDTH:POOL_HIST * POOL_WIDTH]
    for t in range(DEC_SEQ):
        hist_ref[:, (keep_rows + t) * POOL_WIDTH:(keep_rows + t + 1) * POOL_WIDTH] = ext[POOL_HIST + t]


def _proj_sample(hs, w_in_l, wpool_l, pscale_l, sp_l):
    n_qx = 2 * N_HEADS * DEC_SEQ
    out_shape = (
        jax.ShapeDtypeStruct((N_SAMPLE_ROWS, QK_WIDTH), F32),
        jax.ShapeDtypeStruct((N_SAMPLE_ROWS, ATTN_WIDTH), F32),
        jax.ShapeDtypeStruct((n_qx, DEC_BATCH, QK_WIDTH), BF16),
        jax.ShapeDtypeStruct((NEW_PAD, DEC_BATCH, QK_WIDTH), BF16),
        jax.ShapeDtypeStruct((NEW_PAD, DEC_BATCH, ATTN_WIDTH), BF16),
        jax.ShapeDtypeStruct((N_SAMPLE_ROWS, POOL_WIDTH), BF16),
        jax.ShapeDtypeStruct((DEC_BATCH, POOL_HIST * POOL_WIDTH), F32),
    )
    return pl.pallas_call(
        _proj_sample_kernel,
        out_shape=out_shape,
        compiler_params=pltpu.CompilerParams(vmem_limit_bytes=VMEM_LIMIT_BYTES),
        name="proj_sample",
    )(hs, w_in_l, wpool_l, pscale_l, sp_l)


def _sample_attn_kernel(pt_ref, lq_ref, g_ref, qx_ref, knew_ref, vnew_ref, ck_hbm, cv_hbm,
                        o_ref, kbuf, vbuf, sem, *, layer, lam_init):
    b = pl.program_id(0)
    n_b = pl.num_programs(0)
    slot = b % 2

    def page_copies(seq, sl):
        cps = []
        for i in range(N_PAGES):
            page = pt_ref[seq, i]
            rows = pl.ds(i * PAGE_ROWS, PAGE_ROWS)
            cps.append(pltpu.make_async_copy(ck_hbm.at[layer, page], kbuf.at[sl, rows, :], sem.at[0, sl]))
            cps.append(pltpu.make_async_copy(cv_hbm.at[layer, page], vbuf.at[sl, rows, :], sem.at[1, sl]))
        return cps

    @pl.when(b == 0)
    def _():
        for cp in page_copies(0, 0):
            cp.start()

    @pl.when(b + 1 < n_b)
    def _():
        for cp in page_copies(b + 1, 1 - slot):
            cp.start()

    for cp in page_copies(b, slot):
        cp.wait()

    def heads_on_lanes(buf):
        cur = buf.at[slot]
        parts = [cur[pl.ds(h, PAST_LEN, stride=N_HEADS), :] for h in range(N_HEADS)]
        return jnp.concatenate(parts, axis=1).astype(BF16)

    qx = qx_ref[...]
    n_row = qx.shape[0]
    s_past = lax.dot_general(qx, heads_on_lanes(kbuf), NT_DIMS, preferred_element_type=F32)
    s_new = lax.dot_general(qx, knew_ref[...], NT_DIMS, preferred_element_type=F32)
    tt = lax.broadcasted_iota(jnp.int32, (n_row, NEW_PAD), 0) % DEC_SEQ
    jj = lax.broadcasted_iota(jnp.int32, (n_row, NEW_PAD), 1)
    s_new = jnp.where(jj <= tt, s_new, -jnp.inf)
    m = jnp.maximum(jnp.max(s_past, axis=1, keepdims=True), jnp.max(s_new, axis=1, keepdims=True))
    p_past = jnp.exp(s_past - m)
    p_new = jnp.exp(s_new - m)
    inv = 1.0 / (jnp.sum(p_past, axis=1, keepdims=True) + jnp.sum(p_new, axis=1, keepdims=True))
    r = jnp.dot((p_past * inv).astype(BF16), heads_on_lanes(vbuf), preferred_element_type=F32)
    r = r + jnp.dot((p_new * inv).astype(BF16), vnew_ref[...], preferred_element_type=F32)
    lam = _diff_lambda(lq_ref[...], lam_init)
    half = n_row // 2
    o = r[0:half] - lam * r[half:n_row]
    lane_h = lax.broadcasted_iota(jnp.int32, (half, ATTN_WIDTH), 1) // V_DIM
    row_h = lax.broadcasted_iota(jnp.int32, (half, ATTN_WIDTH), 0) // DEC_SEQ
    o = jnp.where(lane_h == row_h, o, 0.0)
    ms = jnp.sum(o * o, axis=1, keepdims=True) * (1.0 / V_DIM)
    o = o * lax.rsqrt(ms + SUBLN_EPS) * g_ref[...] * (1.0 - lam_init)
    out = o[0:DEC_SEQ]
    for h in range(1, N_HEADS):
        out = out + o[h * DEC_SEQ:(h + 1) * DEC_SEQ]
    o_ref[...] = out.astype(BF16)


def _sample_attn(page_table, lq_l, g_row, qx, knew, vnew, cache_k, cache_v, layer):
    n_qx = qx.shape[1]
    kern = functools.partial(_sample_attn_kernel, layer=layer, lam_init=_lam_init(layer))
    grid_spec = pltpu.PrefetchScalarGridSpec(
        num_scalar_prefetch=1,
        grid=(DEC_BATCH,),
        in_specs=[
            pl.BlockSpec((4, HEAD_DIM), lambda b, pt: (0, 0)),
            pl.BlockSpec((1, ATTN_WIDTH), lambda b, pt: (0, 0)),
            pl.BlockSpec((None, n_qx, QK_WIDTH), lambda b, pt: (b, 0, 0)),
            pl.BlockSpec((None, NEW_PAD, QK_WIDTH), lambda b, pt: (b, 0, 0)),
            pl.BlockSpec((None, NEW_PAD, ATTN_WIDTH), lambda b, pt: (b, 0, 0)),
            pl.BlockSpec(memory_space=pl.ANY),
            pl.BlockSpec(memory_space=pl.ANY),
        ],
        out_specs=pl.BlockSpec((None, DEC_SEQ, ATTN_WIDTH), lambda b, pt: (b, 0, 0)),
        scratch_shapes=[
            pltpu.VMEM((2, N_PAGES * PAGE_ROWS, 2 * HEAD_DIM), F32),
            pltpu.VMEM((2, N_PAGES * PAGE_ROWS, V_DIM), F32),
            pltpu.SemaphoreType.DMA((2, 2)),
        ],
    )
    return pl.pallas_call(
        kern,
        out_shape=jax.ShapeDtypeStruct((DEC_BATCH, DEC_SEQ, ATTN_WIDTH), BF16),
        grid_spec=grid_spec,
        compiler_params=pltpu.CompilerParams(
            dimension_semantics=("arbitrary",),
            vmem_limit_bytes=VMEM_LIMIT_BYTES),
        name="sample_attn",
    )(page_table, lq_l, g_row, qx, knew, vnew, cache_k, cache_v)


def kernel(x_prompt, x_sample, cache_k, cache_v, state_pool, page_table, meta_tokens,
           w_in, w_out, lambda_qk, subln_g, pool_w, pool_scale, w_ffn_in, w_ffn_out,
           ln_g, ln_b):
    assert x_prompt.shape == (BATCH, SEQ, D_MODEL)
    assert x_sample.shape == (DEC_BATCH, DEC_SEQ, D_MODEL)
    n_phys = cache_k.shape[1]
    ck = cache_k.reshape(DEPTH, n_phys, PAGE_ROWS, 2 * HEAD_DIM)
    cv = cache_v.reshape(DEPTH, n_phys, PAGE_ROWS, V_DIM)
    sp = state_pool.reshape(DEPTH, DEC_BATCH, POOL_HIST * POOL_WIDTH)

    w_in_b = w_in.astype(BF16)
    w_out_b = w_out.astype(BF16)
    w_ffn_in_b = w_ffn_in.astype(BF16)
    w_ffn_out_b = w_ffn_out.astype(BF16)
    pool_w_b = pool_w.astype(BF16)

    meta = jnp.broadcast_to(meta_tokens[None], (BATCH, N_META, D_MODEL))
    tail_pad = jnp.zeros((BATCH, L_PAD - L_PROMPT, D_MODEL), F32)
    hp = jnp.concatenate([meta, x_prompt, tail_pad], axis=1)
    hs = jnp.transpose(x_sample, (1, 0, 2)).reshape(N_SAMPLE_ROWS, D_MODEL)

    k_all = jnp.zeros((DEPTH, BATCH, L_PROMPT * N_HEADS, V_DIM), F32)
    v_all = jnp.zeros((DEPTH, BATCH, L_PROMPT * N_HEADS, V_DIM), F32)
    u_p, k_s, v_s, u_s = [], [], [], []
    for l in range(DEPTH):
        pscale = pool_scale[l].reshape(1, POOL_WIDTH)
        g_col = subln_g[l].reshape(V_DIM, 1)
        g_row = jnp.tile(subln_g[l], N_HEADS).reshape(1, ATTN_WIDTH)

        k_all, v_all, kb, vt, qx_p, pooled, hist = _proj_prompt(
            hp, w_in_b[l], pool_w_b[l], pscale, k_all, v_all, l)
        attn = _prompt_attn(lambda_qk[l], g_col, qx_p, kb, vt, l)
        hp = _tail(hp.reshape(BATCH * L_PAD, D_MODEL),
                   attn.reshape(BATCH * L_PAD, ATTN_WIDTH),
                   pooled.reshape(BATCH * L_PAD, POOL_WIDTH),
                   w_out_b[l], w_ffn_in_b[l], w_ffn_out_b[l], ln_g[l], ln_b[l])
        hp = hp.reshape(BATCH, L_PAD, D_MODEL)
        u_p.append(hist)

        kfs, vfs, qx, knew, vnew, pooled_s, hist_s = _proj_sample(hs, w_in_b[l], pool_w_b[l], pscale, sp[l])
        to_seq_major = lambda x: jnp.transpose(x, (1, 0, 2))
        attn_s = _sample_attn(page_table, lambda_qk[l], g_row, to_seq_major(qx), to_seq_major(knew),
                              to_seq_major(vnew), ck, cv, l)
        hs = _tail(hs, to_seq_major(attn_s).reshape(N_SAMPLE_ROWS, ATTN_WIDTH), pooled_s,
                   w_out_b[l], w_ffn_in_b[l], w_ffn_out_b[l], ln_g[l], ln_b[l])
        k_s.append(kfs)
        v_s.append(vfs)
        u_s.append(hist_s)

    def prompt_heads(x):
        return x.reshape(DEPTH, BATCH, L_PROMPT, N_HEADS, 2 * HEAD_DIM)

    def sample_heads(xs):
        x = jnp.stack(xs).reshape(DEPTH, DEC_SEQ, DEC_BATCH, N_HEADS, 2 * HEAD_DIM)
        return jnp.transpose(x, (0, 2, 1, 3, 4))

    y_prompt = hp[:, N_META:L_PROMPT]
    y_sample = jnp.transpose(hs.reshape(DEC_SEQ, DEC_BATCH, D_MODEL), (1, 0, 2))
    return (y_prompt, y_sample, prompt_heads(k_all), prompt_heads(v_all), jnp.stack(u_p),
            sample_heads(k_s), sample_heads(v_s),
            jnp.stack(u_s).reshape(DEPTH, DEC_BATCH, POOL_HIST, POOL_WIDTH))
```

```python
import functools
import math

import jax
import jax.numpy as jnp
from jax import lax
from jax.experimental import pallas as pl
from jax.experimental.pallas import tpu as pltpu

D_MODEL = 1024
BATCH = 2
SEQ = 8192
DEPTH = 4
DEC_BATCH = 128
DEC_SEQ = 4
PAST_LEN = 2048
PAGE_SIZE = 128
N_META = 16
ATTN_WIDTH = 512
POOL_WIDTH = 512
HEAD_DIM = 64
V_DIM = 128
N_HEADS = 4
QK_WIDTH = 512
POOL_WINDOWS = (2, 4, 8, 16)
POOL_GROUP = 128
POOL_HIST = 15
PROJ_WIDTH = 2048
D_FF = 2816
DEEPNORM_ALPHA = (2 * DEPTH) ** 0.25
LN_EPS = 1e-5
SUBLN_EPS = 1e-5
QK_SCALE = HEAD_DIM ** -0.5
LOG2_E = math.log2(math.e)

L_PROMPT = N_META + SEQ
TILE = 512
N_TILES = -(-L_PROMPT // TILE)
L_PAD = N_TILES * TILE
LAST_Q = L_PROMPT - (N_TILES - 1) * TILE
LAST_W = -(-LAST_Q // 128) * 128
V_EXT = V_DIM + 16
N_PAGES = PAST_LEN // PAGE_SIZE
PAGE_ROWS = PAGE_SIZE * N_HEADS
N_SAMPLE_ROWS = DEC_BATCH * DEC_SEQ
TAIL_TILE = 256
SEQ_PER_STEP = 2
HIST_PAD = 16
NEW_PAD = 16
VMEM_LIMIT_BYTES = 56 * 1024 * 1024

F32 = jnp.float32
BF16 = jnp.bfloat16
NT_DIMS = (((1,), (1,)), ((), ()))


def _lam_init(layer):
    return 0.8 - 0.6 * math.exp(-0.3 * layer)


def _diff_lambda(lq, lam_init):
    a = jnp.sum(lq[0:1] * lq[1:2], axis=1, keepdims=True)
    b = jnp.sum(lq[2:3] * lq[3:4], axis=1, keepdims=True)
    return jnp.exp(a) - jnp.exp(b) + lam_init


def _layer_norm(x, g, b):
    mu = jnp.mean(x, axis=-1, keepdims=True)
    xc = x - mu
    var = jnp.mean(xc * xc, axis=-1, keepdims=True)
    return xc * lax.rsqrt(var + LN_EPS) * g + b


def _proj_prompt_kernel(h_ref, w_ref, wpool_ref, pscale_ref, k_all_ref, v_all_ref,
                        kf_ref, vf_ref, kb_ref, vt_ref, qx_ref, pooled_ref, hist_ref, ubuf):
    del k_all_ref, v_all_ref
    j = pl.program_id(1)
    p = jnp.dot(h_ref[0].astype(BF16), w_ref[...], preferred_element_type=F32)
    k = p[:, QK_WIDTH:2 * QK_WIDTH]
    v = p[:, 2 * QK_WIDTH:2 * QK_WIDTH + ATTN_WIDTH]
    u = p[:, 2 * QK_WIDTH + ATTN_WIDTH:]
    for h in range(N_HEADS):
        cols = slice(h * V_DIM, (h + 1) * V_DIM)
        kf_ref[0, 0, pl.ds(h, TILE, stride=N_HEADS), :] = k[:, cols]
        vf_ref[0, 0, pl.ds(h, TILE, stride=N_HEADS), :] = v[:, cols]
    kb_ref[0] = k.astype(BF16)
    vt = v.T.astype(BF16)
    ones = jnp.ones((V_EXT - V_DIM, TILE), BF16)
    for h in range(N_HEADS):
        vt_ref[0, 0, h * V_EXT:h * V_EXT + V_DIM, :] = vt[h * V_DIM:(h + 1) * V_DIM]
        vt_ref[0, 0, h * V_EXT + V_DIM:(h + 1) * V_EXT, :] = ones
    qt = (p[:, :QK_WIDTH] * (QK_SCALE * LOG2_E)).T.astype(BF16)
    zeros = jnp.zeros((HEAD_DIM, TILE), BF16)
    for h in range(N_HEADS):
        r = h * 2 * HEAD_DIM
        qx_ref[0, 0, r:r + HEAD_DIM, 0:TILE] = qt[r:r + HEAD_DIM]
        qx_ref[0, 0, r + HEAD_DIM:r + 2 * HEAD_DIM, 0:TILE] = zeros
        qx_ref[0, 0, r:r + HEAD_DIM, TILE:2 * TILE] = zeros
        qx_ref[0, 0, r + HEAD_DIM:r + 2 * HEAD_DIM, TILE:2 * TILE] = qt[r + HEAD_DIM:r + 2 * HEAD_DIM]

    @pl.when(j == 0)
    def _():
        ubuf[0:HIST_PAD, :] = jnp.zeros((HIST_PAD, POOL_WIDTH), F32)

    @pl.when(j > 0)
    def _():
        ubuf[0:HIST_PAD, :] = ubuf[TILE:TILE + HIST_PAD, :]

    ubuf[HIST_PAD:HIST_PAD + TILE, :] = u
    pos = j * TILE + lax.broadcasted_iota(jnp.int32, (TILE, POOL_GROUP), 0)
    outs = []
    for g, win in enumerate(POOL_WINDOWS):
        c0 = g * POOL_GROUP
        s = ubuf[HIST_PAD:HIST_PAD + TILE, c0:c0 + POOL_GROUP]
        for i in range(1, win):
            s = s + ubuf[HIST_PAD - i:HIST_PAD - i + TILE, c0:c0 + POOL_GROUP]
        cnt = jnp.minimum(pos + 1, win).astype(F32)
        m = s / cnt - u[:, c0:c0 + POOL_GROUP]
        outs.append(jnp.dot(m.astype(BF16), wpool_ref[g], preferred_element_type=F32))
    pooled_ref[0] = (jnp.concatenate(outs, axis=-1) * pscale_ref[...]).astype(BF16)

    @pl.when(j == N_TILES - 1)
    def _():
        r0 = HIST_PAD + (L_PROMPT - POOL_HIST) - (N_TILES - 1) * TILE
        hist_ref[0] = ubuf[r0:r0 + POOL_HIST, :]


def _proj_prompt(hp, w_in_l, wpool_l, pscale_l, k_all, v_all, layer):
    grid = (BATCH, N_TILES)
    row = lambda b, j: (b, j, 0)
    const2 = lambda b, j: (0, 0)
    const3 = lambda b, j: (0, 0, 0)
    kv_shape = jax.ShapeDtypeStruct((DEPTH, BATCH, L_PROMPT * N_HEADS, V_DIM), F32)
    kv_spec = pl.BlockSpec((1, 1, TILE * N_HEADS, V_DIM), lambda b, j: (layer, b, j, 0))
    out_shape = (
        kv_shape,
        kv_shape,
        jax.ShapeDtypeStruct((BATCH, L_PAD, QK_WIDTH), BF16),
        jax.ShapeDtypeStruct((BATCH, N_TILES, N_HEADS * V_EXT, TILE), BF16),
        jax.ShapeDtypeStruct((BATCH, N_TILES, QK_WIDTH, 2 * TILE), BF16),
        jax.ShapeDtypeStruct((BATCH, L_PAD, POOL_WIDTH), BF16),
        jax.ShapeDtypeStruct((BATCH, POOL_HIST, POOL_WIDTH), F32),
    )
    out_specs = (
        kv_spec,
        kv_spec,
        pl.BlockSpec((1, TILE, QK_WIDTH), row),
        pl.BlockSpec((1, 1, N_HEADS * V_EXT, TILE), lambda b, j: (b, j, 0, 0)),
        pl.BlockSpec((1, 1, QK_WIDTH, 2 * TILE), lambda b, j: (b, j, 0, 0)),
        pl.BlockSpec((1, TILE, POOL_WIDTH), row),
        pl.BlockSpec((1, POOL_HIST, POOL_WIDTH), lambda b, j: (b, 0, 0)),
    )
    in_specs = [
        pl.BlockSpec((1, TILE, D_MODEL), row),
        pl.BlockSpec((D_MODEL, PROJ_WIDTH), const2),
        pl.BlockSpec((len(POOL_WINDOWS), POOL_GROUP, POOL_GROUP), const3),
        pl.BlockSpec((1, POOL_WIDTH), const2),
        pl.BlockSpec(memory_space=pl.ANY),
        pl.BlockSpec(memory_space=pl.ANY),
    ]
    return pl.pallas_call(
        _proj_prompt_kernel,
        out_shape=out_shape,
        grid=grid,
        in_specs=in_specs,
        out_specs=out_specs,
        scratch_shapes=[pltpu.VMEM((HIST_PAD + TILE, POOL_WIDTH), F32)],
        input_output_aliases={4: 0, 5: 1},
        compiler_params=pltpu.CompilerParams(
            dimension_semantics=("arbitrary", "arbitrary"),
            vmem_limit_bytes=VMEM_LIMIT_BYTES),
        name="proj_prompt",
    )(hp, w_in_l, wpool_l, pscale_l, k_all, v_all)


def _prompt_attn_tile(width, lq_ref, g_ref, qx_ref, k_ref, vt_ref, o_ref,
                      s0_sc, s1_sc, mx0_sc, mx1_sc, m_sc, acc_sc, *, lam_init):
    qi = pl.program_id(2)
    slots = ((s0_sc, mx0_sc), (s1_sc, mx1_sc))
    lanes = slice(0, 2 * width)
    if width == TILE:
        q = qx_ref[0, 0]
    else:
        q = jnp.concatenate([qx_ref[0, 0, :, 0:width], qx_ref[0, 0, :, TILE:TILE + width]], axis=1)
    m_sc[:, lanes] = jnp.full((1, 2 * width), -jnp.inf, F32)
    acc_sc[:, lanes] = jnp.zeros((V_EXT, 2 * width), F32)

    def scores(j, slot):
        s_sc, mx_sc = slots[slot]
        k = k_ref[0, pl.ds(pl.multiple_of(j * TILE, TILE), TILE), :]
        s = jnp.dot(k, q, preferred_element_type=F32)
        s_sc[:, lanes] = s
        mx_sc[:, lanes] = jnp.max(s, axis=0, keepdims=True)

    def softmax_pv(j, slot, masked):
        s_sc, mx_sc = slots[slot]
        s = s_sc[:, lanes]
        if masked:
            kk = lax.broadcasted_iota(jnp.int32, (TILE, 2 * width), 0)
            qq = lax.broadcasted_iota(jnp.int32, (TILE, 2 * width), 1) & (width - 1)
            s = jnp.where(kk <= qq, s, -jnp.inf)
            mx = jnp.max(s, axis=0, keepdims=True)
        else:
            mx = mx_sc[:, lanes]
        m_old = m_sc[:, lanes]
        m_new = jnp.maximum(m_old, mx)
        a = jnp.exp2(m_old - m_new)
        p = jnp.exp2(s - m_new).astype(BF16)
        acc_sc[:, lanes] = a * acc_sc[:, lanes] + jnp.dot(vt_ref[0, j], p, preferred_element_type=F32)
        m_sc[:, lanes] = m_new

    scores(0, 0)

    def pair(jj, carry):
        j = 2 * jj
        scores(j + 1, 1)
        softmax_pv(j, 0, False)
        scores(j + 2, 0)
        softmax_pv(j + 1, 1, False)
        return carry

    lax.fori_loop(0, qi // 2, pair, 0)

    @pl.when(qi % 2 == 0)
    def _():
        softmax_pv(qi, 0, True)

    @pl.when(qi % 2 == 1)
    def _():
        scores(qi, 1)
        softmax_pv(qi - 1, 0, False)
        softmax_pv(qi, 1, True)

    lam = _diff_lambda(lq_ref[...], lam_init)
    inv = 1.0 / acc_sc[V_DIM:V_DIM + 1, lanes]
    o = (acc_sc[0:V_DIM, 0:width] * inv[:, 0:width]
         - lam * (acc_sc[0:V_DIM, width:2 * width] * inv[:, width:2 * width]))
    ms = jnp.mean(o * o, axis=0, keepdims=True)
    o = o * lax.rsqrt(ms + SUBLN_EPS) * g_ref[...] * (1.0 - lam_init)
    o_ref[0, 0:width, :] = o.T.astype(BF16)
    if width < TILE:
        o_ref[0, width:TILE, :] = jnp.zeros((TILE - width, V_DIM), BF16)


def _prompt_attn_kernel(*refs, lam_init):
    qi = pl.program_id(2)

    @pl.when(qi < N_TILES - 1)
    def _():
        _prompt_attn_tile(TILE, *refs, lam_init=lam_init)

    @pl.when(qi == N_TILES - 1)
    def _():
        _prompt_attn_tile(LAST_W, *refs, lam_init=lam_init)


def _prompt_attn(lq_l, g_col, qx, kb, vt, layer):
    grid = (BATCH, N_HEADS, N_TILES)
    kern = functools.partial(_prompt_attn_kernel, lam_init=_lam_init(layer))
    return pl.pallas_call(
        kern,
        out_shape=jax.ShapeDtypeStruct((BATCH, L_PAD, ATTN_WIDTH), BF16),
        grid=grid,
        in_specs=[
            pl.BlockSpec((4, HEAD_DIM), lambda b, h, i: (0, 0)),
            pl.BlockSpec((V_DIM, 1), lambda b, h, i: (0, 0)),
            pl.BlockSpec((1, 1, 2 * HEAD_DIM, 2 * TILE), lambda b, h, i: (b, i, h, 0)),
            pl.BlockSpec((1, L_PAD, 2 * HEAD_DIM), lambda b, h, i: (b, 0, h)),
            pl.BlockSpec((1, N_TILES, V_EXT, TILE), lambda b, h, i: (b, 0, h, 0)),
        ],
        out_specs=pl.BlockSpec((1, TILE, V_DIM), lambda b, h, i: (b, i, h)),
        scratch_shapes=[
            pltpu.VMEM((TILE, 2 * TILE), F32),
            pltpu.VMEM((TILE, 2 * TILE), F32),
            pltpu.VMEM((1, 2 * TILE), F32),
            pltpu.VMEM((1, 2 * TILE), F32),
            pltpu.VMEM((1, 2 * TILE), F32),
            pltpu.VMEM((V_EXT, 2 * TILE), F32),
        ],
        compiler_params=pltpu.CompilerParams(
            dimension_semantics=("arbitrary", "arbitrary", "arbitrary"),
            vmem_limit_bytes=VMEM_LIMIT_BYTES),
        name="prompt_attn",
    )(lq_l, g_col, qx, kb, vt)


def _tail_rows(h_ref, a_ref, p_ref, wo_ref, wfi_ref, wfo_ref, g_ref, b_ref, o_ref):
    h = h_ref[...]
    x = jnp.concatenate([a_ref[...], p_ref[...]], axis=-1)
    mix = jnp.dot(x, wo_ref[...], preferred_element_type=F32)
    h1 = _layer_norm(DEEPNORM_ALPHA * h + mix, g_ref[0:1], b_ref[0:1])
    gu = jnp.dot(h1.astype(BF16), wfi_ref[...], preferred_element_type=F32)
    gate = gu[:, :D_FF]
    act = gate * jax.nn.sigmoid(gate) * gu[:, D_FF:]
    y = jnp.dot(act.astype(BF16), wfo_ref[...], preferred_element_type=F32)
    o_ref[...] = _layer_norm(DEEPNORM_ALPHA * h1 + y, g_ref[1:2], b_ref[1:2])


def _tail(h, attn, pooled, wo_l, wfi_l, wfo_l, g_l, b_l):
    n_rows = h.shape[0]
    grid = (n_rows // TILE,)
    row = lambda i: (i, 0)
    const = lambda i: (0, 0)
    once = pl.Buffered(1)
    return pl.pallas_call(
        _tail_rows,
        out_shape=jax.ShapeDtypeStruct((n_rows, D_MODEL), F32),
        grid=grid,
        in_specs=[
            pl.BlockSpec((TILE, D_MODEL), row),
            pl.BlockSpec((TILE, ATTN_WIDTH), row),
            pl.BlockSpec((TILE, POOL_WIDTH), row),
            pl.BlockSpec((D_MODEL, D_MODEL), const, pipeline_mode=once),
            pl.BlockSpec((D_MODEL, 2 * D_FF), const, pipeline_mode=once),
            pl.BlockSpec((D_FF, D_MODEL), const, pipeline_mode=once),
            pl.BlockSpec((2, D_MODEL), const),
            pl.BlockSpec((2, D_MODEL), const),
        ],
        out_specs=pl.BlockSpec((TILE, D_MODEL), row),
        compiler_params=pltpu.CompilerParams(
            dimension_semantics=("arbitrary",),
            vmem_limit_bytes=VMEM_LIMIT_BYTES),
        name="layer_tail",
    )(h, attn, pooled, wo_l, wfi_l, wfo_l, g_l, b_l)


def _proj_sample_kernel(h_ref, w_ref, wpool_ref, pscale_ref, sp_ref,
                        kf_ref, vf_ref, qx_ref, knew_ref, vnew_ref, pooled_ref, hist_ref):
    p = jnp.dot(h_ref[...].astype(BF16), w_ref[...], preferred_element_type=F32)
    q = p[:, :QK_WIDTH] * QK_SCALE
    k = p[:, QK_WIDTH:2 * QK_WIDTH]
    v = p[:, 2 * QK_WIDTH:2 * QK_WIDTH + ATTN_WIDTH]
    u = p[:, 2 * QK_WIDTH + ATTN_WIDTH:]
    kf_ref[...] = k
    vf_ref[...] = v
    lane = lax.broadcasted_iota(jnp.int32, (DEC_BATCH, QK_WIDTH), 1)
    for t in range(DEC_SEQ):
        rows = slice(t * DEC_BATCH, (t + 1) * DEC_BATCH)
        qt = q[rows]
        for c in range(2):
            for h in range(N_HEADS):
                lo = h * 2 * HEAD_DIM + c * HEAD_DIM
                keep = (lane >= lo) & (lane < lo + HEAD_DIM)
                qx_ref[c * N_HEADS * DEC_SEQ + h * DEC_SEQ + t] = jnp.where(keep, qt, 0.0).astype(BF16)
        knew_ref[t] = k[rows].astype(BF16)
        vnew_ref[t] = v[rows].astype(BF16)
    pad = jnp.zeros((NEW_PAD - DEC_SEQ, DEC_BATCH, QK_WIDTH), BF16)
    knew_ref[DEC_SEQ:NEW_PAD] = pad
    vnew_ref[DEC_SEQ:NEW_PAD] = pad

    ext = [sp_ref[:, i * POOL_WIDTH:(i + 1) * POOL_WIDTH] for i in range(POOL_HIST)]
    ext += [u[t * DEC_BATCH:(t + 1) * DEC_BATCH] for t in range(DEC_SEQ)]
    for t in range(DEC_SEQ):
        r = POOL_HIST + t
        outs = []
        for g, win in enumerate(POOL_WINDOWS):
            cols = slice(g * POOL_GROUP, (g + 1) * POOL_GROUP)
            s = ext[r][:, cols]
            for i in range(1, win):
                s = s + ext[r - i][:, cols]
            m = s / float(win) - ext[r][:, cols]
            outs.append(jnp.dot(m.astype(BF16), wpool_ref[g], preferred_element_type=F32))
        pooled_ref[t * DEC_BATCH:(t + 1) * DEC_BATCH, :] = (
            jnp.concatenate(outs, axis=-1) * pscale_ref[...]).astype(BF16)
    keep_rows = POOL_HIST - DEC_SEQ
    hist_ref[:, 0:keep_rows * POOL_WIDTH] = sp_ref[:, DEC_SEQ * POOL_WIDTH:POOL_HIST * POOL_WIDTH]
    for t in range(DEC_SEQ):
        hist_ref[:, (keep_rows + t) * POOL_WIDTH:(keep_rows + t + 1) * POOL_WIDTH] = ext[POOL_HIST + t]


def _proj_sample(hs, w_in_l, wpool_l, pscale_l, sp_l):
    n_qx = 2 * N_HEADS * DEC_SEQ
    out_shape = (
        jax.ShapeDtypeStruct((N_SAMPLE_ROWS, QK_WIDTH), F32),
        jax.ShapeDtypeStruct((N_SAMPLE_ROWS, ATTN_WIDTH), F32),
        jax.ShapeDtypeStruct((n_qx, DEC_BATCH, QK_WIDTH), BF16),
        jax.ShapeDtypeStruct((NEW_PAD, DEC_BATCH, QK_WIDTH), BF16),
        jax.ShapeDtypeStruct((NEW_PAD, DEC_BATCH, ATTN_WIDTH), BF16),
        jax.ShapeDtypeStruct((N_SAMPLE_ROWS, POOL_WIDTH), BF16),
        jax.ShapeDtypeStruct((DEC_BATCH, POOL_HIST * POOL_WIDTH), F32),
    )
    return pl.pallas_call(
        _proj_sample_kernel,
        out_shape=out_shape,
        compiler_params=pltpu.CompilerParams(vmem_limit_bytes=VMEM_LIMIT_BYTES),
        name="proj_sample",
    )(hs, w_in_l, wpool_l, pscale_l, sp_l)


def _sample_attend(kview, vview, qx, knew, vnew, lq, g_row, lam_init):
    def heads_on_lanes(view):
        parts = [view[pl.ds(h, PAST_LEN, stride=N_HEADS), :] for h in range(N_HEADS)]
        return jnp.concatenate(parts, axis=1).astype(BF16)

    n_row = qx.shape[0]
    s_past = lax.dot_general(qx, heads_on_lanes(kview), NT_DIMS, preferred_element_type=F32)
    s_new = lax.dot_general(qx, knew, NT_DIMS, preferred_element_type=F32)
    tt = lax.broadcasted_iota(jnp.int32, (n_row, NEW_PAD), 0) % DEC_SEQ
    jj = lax.broadcasted_iota(jnp.int32, (n_row, NEW_PAD), 1)
    s_new = jnp.where(jj <= tt, s_new, -jnp.inf)
    m = jnp.maximum(jnp.max(s_past, axis=1, keepdims=True), jnp.max(s_new, axis=1, keepdims=True))
    p_past = jnp.exp(s_past - m)
    p_new = jnp.exp(s_new - m)
    inv = 1.0 / (jnp.sum(p_past, axis=1, keepdims=True) + jnp.sum(p_new, axis=1, keepdims=True))
    r = jnp.dot((p_past * inv).astype(BF16), heads_on_lanes(vview), preferred_element_type=F32)
    r = r + jnp.dot((p_new * inv).astype(BF16), vnew, preferred_element_type=F32)
    lam = _diff_lambda(lq, lam_init)
    half = n_row // 2
    o = r[0:half] - lam * r[half:n_row]
    lane_h = lax.broadcasted_iota(jnp.int32, (half, ATTN_WIDTH), 1) // V_DIM
    row_h = lax.broadcasted_iota(jnp.int32, (half, ATTN_WIDTH), 0) // DEC_SEQ
    o = jnp.where(lane_h == row_h, o, 0.0)
    ms = jnp.sum(o * o, axis=1, keepdims=True) * (1.0 / V_DIM)
    o = o * lax.rsqrt(ms + SUBLN_EPS) * g_row * (1.0 - lam_init)
    out = o[0:DEC_SEQ]
    for h in range(1, N_HEADS):
        out = out + o[h * DEC_SEQ:(h + 1) * DEC_SEQ]
    return out.astype(BF16)


def _tail_sample_kernel(pt_ref, h_ref, a_ref, p_ref, wo_ref, wfi_ref, wfo_ref, g_ref, b_ref,
                        lq_ref, gs_ref, qx_ref, knew_ref, vnew_ref, ck_hbm, cv_hbm,
                        o_ref, os_ref, kbuf, vbuf, sem, *, layer, lam_init):
    i = pl.program_id(0)
    n_steps = pl.num_programs(0)

    def seq_of(step, s):
        return jnp.minimum(step, DEC_BATCH // SEQ_PER_STEP - 1) * SEQ_PER_STEP + s

    def page_copies(step, s):
        seq = seq_of(step, s)
        cps = []
        for pg in range(N_PAGES):
            page = pt_ref[seq, pg]
            rows = pl.ds(pg * PAGE_ROWS, PAGE_ROWS)
            cps.append(pltpu.make_async_copy(ck_hbm.at[layer, page], kbuf.at[s, rows, :], sem.at[0, s]))
            cps.append(pltpu.make_async_copy(cv_hbm.at[layer, page], vbuf.at[s, rows, :], sem.at[1, s]))
        return cps

    @pl.when(i == 0)
    def _():
        for s in range(SEQ_PER_STEP):
            for cp in page_copies(0, s):
                cp.start()

    for s in range(SEQ_PER_STEP):
        for cp in page_copies(i, s):
            cp.wait()
    for s in range(SEQ_PER_STEP):
        os_ref[s] = _sample_attend(kbuf.at[s], vbuf.at[s], qx_ref[s], knew_ref[s], vnew_ref[s],
                                   lq_ref[...], gs_ref[...], lam_init)
    for s in range(SEQ_PER_STEP):
        for cp in page_copies(i + 1, s):
            cp.start()

    _tail_rows(h_ref, a_ref, p_ref, wo_ref, wfi_ref, wfo_ref, g_ref, b_ref, o_ref)

    @pl.when(i == n_steps - 1)
    def _():
        for s in range(SEQ_PER_STEP):
            for cp in page_copies(i + 1, s):
                cp.wait()


def _tail_sample(page_table, h, attn, pooled, wo_l, wfi_l, wfo_l, g_l, b_l,
                 lq_l, g_row, qx, knew, vnew, cache_k, cache_v, layer):
    n_rows = h.shape[0]
    n_steps = n_rows // TAIL_TILE
    n_seq_steps = DEC_BATCH // SEQ_PER_STEP
    assert n_steps >= n_seq_steps and DEC_BATCH % SEQ_PER_STEP == 0
    n_qx = qx.shape[1]
    kern = functools.partial(_tail_sample_kernel, layer=layer, lam_init=_lam_init(layer))
    row = lambda i, pt: (i, 0)
    const = lambda i, pt: (0, 0)
    seqs = lambda i, pt: (jnp.minimum(i, n_seq_steps - 1), 0, 0)
    once = pl.Buffered(1)
    grid_spec = pltpu.PrefetchScalarGridSpec(
        num_scalar_prefetch=1,
        grid=(n_steps,),
        in_specs=[
            pl.BlockSpec((TAIL_TILE, D_MODEL), row),
            pl.BlockSpec((TAIL_TILE, ATTN_WIDTH), row),
            pl.BlockSpec((TAIL_TILE, POOL_WIDTH), row),
            pl.BlockSpec((D_MODEL, D_MODEL), const, pipeline_mode=once),
            pl.BlockSpec((D_MODEL, 2 * D_FF), const, pipeline_mode=once),
            pl.BlockSpec((D_FF, D_MODEL), const, pipeline_mode=once),
            pl.BlockSpec((2, D_MODEL), const),
            pl.BlockSpec((2, D_MODEL), const),
            pl.BlockSpec((4, HEAD_DIM), const),
            pl.BlockSpec((1, ATTN_WIDTH), const),
            pl.BlockSpec((SEQ_PER_STEP, n_qx, QK_WIDTH), seqs),
            pl.BlockSpec((SEQ_PER_STEP, NEW_PAD, QK_WIDTH), seqs),
            pl.BlockSpec((SEQ_PER_STEP, NEW_PAD, ATTN_WIDTH), seqs),
            pl.BlockSpec(memory_space=pl.ANY),
            pl.BlockSpec(memory_space=pl.ANY),
        ],
        out_specs=(
            pl.BlockSpec((TAIL_TILE, D_MODEL), row),
            pl.BlockSpec((SEQ_PER_STEP, DEC_SEQ, ATTN_WIDTH), seqs),
        ),
        scratch_shapes=[
            pltpu.VMEM((SEQ_PER_STEP, N_PAGES * PAGE_ROWS, 2 * HEAD_DIM), F32),
            pltpu.VMEM((SEQ_PER_STEP, N_PAGES * PAGE_ROWS, V_DIM), F32),
            pltpu.SemaphoreType.DMA((2, SEQ_PER_STEP)),
        ],
    )
    return pl.pallas_call(
        kern,
        out_shape=(
            jax.ShapeDtypeStruct((n_rows, D_MODEL), F32),
            jax.ShapeDtypeStruct((DEC_BATCH, DEC_SEQ, ATTN_WIDTH), BF16),
        ),
        grid_spec=grid_spec,
        compiler_params=pltpu.CompilerParams(
            dimension_semantics=("arbitrary",),
            vmem_limit_bytes=VMEM_LIMIT_BYTES),
        name="tail_sample_attn",
    )(page_table, h, attn, pooled, wo_l, wfi_l, wfo_l, g_l, b_l,
      lq_l, g_row, qx, knew, vnew, cache_k, cache_v)


def kernel(x_prompt, x_sample, cache_k, cache_v, state_pool, page_table, meta_tokens,
           w_in, w_out, lambda_qk, subln_g, pool_w, pool_scale, w_ffn_in, w_ffn_out,
           ln_g, ln_b):
    assert x_prompt.shape == (BATCH, SEQ, D_MODEL)
    assert x_sample.shape == (DEC_BATCH, DEC_SEQ, D_MODEL)
    n_phys = cache_k.shape[1]
    ck = cache_k.reshape(DEPTH, n_phys, PAGE_ROWS, 2 * HEAD_DIM)
    cv = cache_v.reshape(DEPTH, n_phys, PAGE_ROWS, V_DIM)
    sp = state_pool.reshape(DEPTH, DEC_BATCH, POOL_HIST * POOL_WIDTH)

    w_in_b = w_in.astype(BF16)
    w_out_b = w_out.astype(BF16)
    w_ffn_in_b = w_ffn_in.astype(BF16)
    w_ffn_out_b = w_ffn_out.astype(BF16)
    pool_w_b = pool_w.astype(BF16)

    meta = jnp.broadcast_to(meta_tokens[None], (BATCH, N_META, D_MODEL))
    tail_pad = jnp.zeros((BATCH, L_PAD - L_PROMPT, D_MODEL), F32)
    hp = jnp.concatenate([meta, x_prompt, tail_pad], axis=1)
    hs = jnp.transpose(x_sample, (1, 0, 2)).reshape(N_SAMPLE_ROWS, D_MODEL)

    k_all = jnp.zeros((DEPTH, BATCH, L_PROMPT * N_HEADS, V_DIM), F32)
    v_all = jnp.zeros((DEPTH, BATCH, L_PROMPT * N_HEADS, V_DIM), F32)
    u_p, k_s, v_s, u_s = [], [], [], []
    for l in range(DEPTH):
        pscale = pool_scale[l].reshape(1, POOL_WIDTH)
        g_col = subln_g[l].reshape(V_DIM, 1)
        g_row = jnp.tile(subln_g[l], N_HEADS).reshape(1, ATTN_WIDTH)

        k_all, v_all, kb, vt, qx_p, pooled, hist = _proj_prompt(
            hp, w_in_b[l], pool_w_b[l], pscale, k_all, v_all, l)
        attn = _prompt_attn(lambda_qk[l], g_col, qx_p, kb, vt, l)
        u_p.append(hist)

        kfs, vfs, qx, knew, vnew, pooled_s, hist_s = _proj_sample(hs, w_in_b[l], pool_w_b[l], pscale, sp[l])
        to_seq_major = lambda x: jnp.transpose(x, (1, 0, 2))
        hp, attn_s = _tail_sample(
            page_table, hp.reshape(BATCH * L_PAD, D_MODEL),
            attn.reshape(BATCH * L_PAD, ATTN_WIDTH), pooled.reshape(BATCH * L_PAD, POOL_WIDTH),
            w_out_b[l], w_ffn_in_b[l], w_ffn_out_b[l], ln_g[l], ln_b[l],
            lambda_qk[l], g_row, to_seq_major(qx), to_seq_major(knew), to_seq_major(vnew), ck, cv, l)
        hp = hp.reshape(BATCH, L_PAD, D_MODEL)
        hs = _tail(hs, to_seq_major(attn_s).reshape(N_SAMPLE_ROWS, ATTN_WIDTH), pooled_s,
                   w_out_b[l], w_ffn_in_b[l], w_ffn_out_b[l], ln_g[l], ln_b[l])
        k_s.append(kfs)
        v_s.append(vfs)
        u_s.append(hist_s)

    def prompt_heads(x):
        return x.reshape(DEPTH, BATCH, L_PROMPT, N_HEADS, 2 * HEAD_DIM)

    def sample_heads(xs):
        x = jnp.stack(xs).reshape(DEPTH, DEC_SEQ, DEC_BATCH, N_HEADS, 2 * HEAD_DIM)
        return jnp.transpose(x, (0, 2, 1, 3, 4))

    y_prompt = hp[:, N_META:L_PROMPT]
    y_sample = jnp.transpose(hs.reshape(DEC_SEQ, DEC_BATCH, D_MODEL), (1, 0, 2))
    return (y_prompt, y_sample, prompt_heads(k_all), prompt_heads(v_all), jnp.stack(u_p),
            sample_heads(k_s), sample_heads(v_s),
            jnp.stack(u_s).reshape(DEPTH, DEC_BATCH, POOL_HIST, POOL_WIDTH))
```

```python
import functools
import math

import jax
import jax.numpy as jnp
from jax import lax
from jax.experimental import pallas as pl
from jax.experimental.pallas import tpu as pltpu

D_MODEL = 1024
BATCH = 2
SEQ = 8192
DEPTH = 4
DEC_BATCH = 128
DEC_SEQ = 4
PAST_LEN = 2048
PAGE_SIZE = 128
N_META = 16
ATTN_WIDTH = 512
POOL_WIDTH = 512
HEAD_DIM = 64
V_DIM = 128
N_HEADS = 4
QK_WIDTH = 512
POOL_WINDOWS = (2, 4, 8, 16)
POOL_GROUP = 128
POOL_HIST = 15
PROJ_WIDTH = 2048
D_FF = 2816
DEEPNORM_ALPHA = (2 * DEPTH) ** 0.25
LN_EPS = 1e-5
SUBLN_EPS = 1e-5
QK_SCALE = HEAD_DIM ** -0.5
LOG2_E = math.log2(math.e)

L_PROMPT = N_META + SEQ
TILE = 512
N_TILES = -(-L_PROMPT // TILE)
L_PAD = N_TILES * TILE
LAST_Q = L_PROMPT - (N_TILES - 1) * TILE
LAST_W = -(-LAST_Q // 128) * 128
V_EXT = V_DIM + 16
HEADS_PER_STEP = 2
N_PAGES = PAST_LEN // PAGE_SIZE
PAGE_ROWS = PAGE_SIZE * N_HEADS
N_SAMPLE_ROWS = DEC_BATCH * DEC_SEQ
TAIL_TILE = 256
SEQ_PER_STEP = 2
HIST_PAD = 16
NEW_PAD = 16
VMEM_LIMIT_BYTES = 56 * 1024 * 1024

F32 = jnp.float32
BF16 = jnp.bfloat16
NT_DIMS = (((1,), (1,)), ((), ()))


def _lam_init(layer):
    return 0.8 - 0.6 * math.exp(-0.3 * layer)


def _diff_lambda(lq, lam_init):
    a = jnp.sum(lq[0:1] * lq[1:2], axis=1, keepdims=True)
    b = jnp.sum(lq[2:3] * lq[3:4], axis=1, keepdims=True)
    return jnp.exp(a) - jnp.exp(b) + lam_init


def _layer_norm(x, g, b):
    mu = jnp.mean(x, axis=-1, keepdims=True)
    xc = x - mu
    var = jnp.mean(xc * xc, axis=-1, keepdims=True)
    return xc * lax.rsqrt(var + LN_EPS) * g + b


def _proj_prompt_kernel(*refs, first_layer):
    h_ref, w_ref, wpool_ref, pscale_ref = refs[:4]
    kf_ref, vf_ref, kb_ref, vt_ref, qx_ref, pooled_ref, hist_ref, ubuf = refs[4 if first_layer else 6:]
    if first_layer:
        later = jnp.zeros((DEPTH - 1, TILE * N_HEADS, V_DIM), F32)
        kf_ref[1:DEPTH, 0] = later
        vf_ref[1:DEPTH, 0] = later
    j = pl.program_id(1)
    p = jnp.dot(h_ref[0].astype(BF16), w_ref[...], preferred_element_type=F32)
    k = p[:, QK_WIDTH:2 * QK_WIDTH]
    v = p[:, 2 * QK_WIDTH:2 * QK_WIDTH + ATTN_WIDTH]
    u = p[:, 2 * QK_WIDTH + ATTN_WIDTH:]
    for h in range(N_HEADS):
        cols = slice(h * V_DIM, (h + 1) * V_DIM)
        kf_ref[0, 0, pl.ds(h, TILE, stride=N_HEADS), :] = k[:, cols]
        vf_ref[0, 0, pl.ds(h, TILE, stride=N_HEADS), :] = v[:, cols]
    kb_ref[0] = k.astype(BF16)
    vt = v.T.astype(BF16)
    ones = jnp.ones((V_EXT - V_DIM, TILE), BF16)
    for h in range(N_HEADS):
        vt_ref[0, 0, h * V_EXT:h * V_EXT + V_DIM, :] = vt[h * V_DIM:(h + 1) * V_DIM]
        vt_ref[0, 0, h * V_EXT + V_DIM:(h + 1) * V_EXT, :] = ones
    qt = (p[:, :QK_WIDTH] * (QK_SCALE * LOG2_E)).T.astype(BF16)
    zeros = jnp.zeros((HEAD_DIM, TILE), BF16)
    for h in range(N_HEADS):
        r = h * 2 * HEAD_DIM
        qx_ref[0, 0, r:r + HEAD_DIM, 0:TILE] = qt[r:r + HEAD_DIM]
        qx_ref[0, 0, r + HEAD_DIM:r + 2 * HEAD_DIM, 0:TILE] = zeros
        qx_ref[0, 0, r:r + HEAD_DIM, TILE:2 * TILE] = zeros
        qx_ref[0, 0, r + HEAD_DIM:r + 2 * HEAD_DIM, TILE:2 * TILE] = qt[r + HEAD_DIM:r + 2 * HEAD_DIM]

    @pl.when(j == 0)
    def _():
        ubuf[0:HIST_PAD, :] = jnp.zeros((HIST_PAD, POOL_WIDTH), F32)

    @pl.when(j > 0)
    def _():
        ubuf[0:HIST_PAD, :] = ubuf[TILE:TILE + HIST_PAD, :]

    ubuf[HIST_PAD:HIST_PAD + TILE, :] = u
    pos = j * TILE + lax.broadcasted_iota(jnp.int32, (TILE, POOL_GROUP), 0)
    outs = []
    for g, win in enumerate(POOL_WINDOWS):
        c0 = g * POOL_GROUP
        s = ubuf[HIST_PAD:HIST_PAD + TILE, c0:c0 + POOL_GROUP]
        for i in range(1, win):
            s = s + ubuf[HIST_PAD - i:HIST_PAD - i + TILE, c0:c0 + POOL_GROUP]
        cnt = jnp.minimum(pos + 1, win).astype(F32)
        m = s / cnt - u[:, c0:c0 + POOL_GROUP]
        outs.append(jnp.dot(m.astype(BF16), wpool_ref[g], preferred_element_type=F32))
    pooled_ref[0] = (jnp.concatenate(outs, axis=-1) * pscale_ref[...]).astype(BF16)

    @pl.when(j == N_TILES - 1)
    def _():
        r0 = HIST_PAD + (L_PROMPT - POOL_HIST) - (N_TILES - 1) * TILE
        hist_ref[0] = ubuf[r0:r0 + POOL_HIST, :]


def _proj_prompt(hp, w_in_all, wpool_all, pscale_l, k_all, v_all, layer):
    first_layer = k_all is None
    assert first_layer == (layer == 0)
    grid = (BATCH, N_TILES)
    row = lambda b, j: (b, j, 0)
    const2 = lambda b, j: (0, 0)
    kv_shape = jax.ShapeDtypeStruct((DEPTH, BATCH, L_PROMPT * N_HEADS, V_DIM), F32)
    kv_layers = DEPTH if first_layer else 1
    kv_spec = pl.BlockSpec((kv_layers, 1, TILE * N_HEADS, V_DIM), lambda b, j: (layer, b, j, 0))
    out_shape = (
        kv_shape,
        kv_shape,
        jax.ShapeDtypeStruct((BATCH, L_PAD, QK_WIDTH), BF16),
        jax.ShapeDtypeStruct((BATCH, N_TILES, N_HEADS * V_EXT, TILE), BF16),
        jax.ShapeDtypeStruct((BATCH, N_TILES, QK_WIDTH, 2 * TILE), BF16),
        jax.ShapeDtypeStruct((BATCH, L_PAD, POOL_WIDTH), BF16),
        jax.ShapeDtypeStruct((BATCH, POOL_HIST, POOL_WIDTH), F32),
    )
    out_specs = (
        kv_spec,
        kv_spec,
        pl.BlockSpec((1, TILE, QK_WIDTH), row),
        pl.BlockSpec((1, 1, N_HEADS * V_EXT, TILE), lambda b, j: (b, j, 0, 0)),
        pl.BlockSpec((1, 1, QK_WIDTH, 2 * TILE), lambda b, j: (b, j, 0, 0)),
        pl.BlockSpec((1, TILE, POOL_WIDTH), row),
        pl.BlockSpec((1, POOL_HIST, POOL_WIDTH), lambda b, j: (b, 0, 0)),
    )
    in_specs = [
        pl.BlockSpec((1, TILE, D_MODEL), row),
        pl.BlockSpec((None, D_MODEL, PROJ_WIDTH), lambda b, j: (layer, 0, 0)),
        pl.BlockSpec((None, len(POOL_WINDOWS), POOL_GROUP, POOL_GROUP), lambda b, j: (layer, 0, 0, 0)),
        pl.BlockSpec((1, POOL_WIDTH), const2),
    ]
    args = [hp, w_in_all, wpool_all, pscale_l]
    aliases = {}
    if not first_layer:
        in_specs += [pl.BlockSpec(memory_space=pl.ANY), pl.BlockSpec(memory_space=pl.ANY)]
        args += [k_all, v_all]
        aliases = {4: 0, 5: 1}
    return pl.pallas_call(
        functools.partial(_proj_prompt_kernel, first_layer=first_layer),
        out_shape=out_shape,
        grid=grid,
        in_specs=in_specs,
        out_specs=out_specs,
        scratch_shapes=[pltpu.VMEM((HIST_PAD + TILE, POOL_WIDTH), F32)],
        input_output_aliases=aliases,
        compiler_params=pltpu.CompilerParams(
            dimension_semantics=("arbitrary", "arbitrary"),
            vmem_limit_bytes=VMEM_LIMIT_BYTES),
        name="proj_prompt",
    )(*args)


def _prompt_attn_tile(width, lq_ref, g_ref, qx_ref, k_ref, vt_ref, o_ref,
                      s0_sc, s1_sc, mx0_sc, mx1_sc, m_sc, acc_sc, *, lam_init):
    qi = pl.program_id(2)
    slots = ((s0_sc, mx0_sc), (s1_sc, mx1_sc))
    lanes = slice(0, 2 * width)
    heads = range(HEADS_PER_STEP)
    qs = []
    for hh in heads:
        rows = slice(hh * 2 * HEAD_DIM, (hh + 1) * 2 * HEAD_DIM)
        if width == TILE:
            qs.append(qx_ref[0, 0, rows, :])
        else:
            qs.append(jnp.concatenate([qx_ref[0, 0, rows, 0:width],
                                       qx_ref[0, 0, rows, TILE:TILE + width]], axis=1))
        m_sc[hh, :, lanes] = jnp.full((1, 2 * width), -jnp.inf, F32)
        acc_sc[hh, :, lanes] = jnp.zeros((V_EXT, 2 * width), F32)

    def scores(j, slot):
        s_sc, mx_sc = slots[slot]
        for hh in heads:
            k = k_ref[0, pl.ds(pl.multiple_of(j * TILE, TILE), TILE), hh * 2 * HEAD_DIM:(hh + 1) * 2 * HEAD_DIM]
            s = jnp.dot(k, qs[hh], preferred_element_type=F32)
            s_sc[hh, :, lanes] = s
            mx_sc[hh, :, lanes] = jnp.max(s, axis=0, keepdims=True)

    def softmax_pv(j, slot, masked):
        s_sc, mx_sc = slots[slot]
        for hh in heads:
            s = s_sc[hh, :, lanes]
            if masked:
                kk = lax.broadcasted_iota(jnp.int32, (TILE, 2 * width), 0)
                qq = lax.broadcasted_iota(jnp.int32, (TILE, 2 * width), 1) & (width - 1)
                s = jnp.where(kk <= qq, s, -jnp.inf)
                mx = jnp.max(s, axis=0, keepdims=True)
            else:
                mx = mx_sc[hh, :, lanes]
            m_old = m_sc[hh, :, lanes]
            m_new = jnp.maximum(m_old, mx)
            a = jnp.exp2(m_old - m_new)
            p = jnp.exp2(s - m_new).astype(BF16)
            vt = vt_ref[0, j, hh * V_EXT:(hh + 1) * V_EXT, :]
            acc_sc[hh, :, lanes] = a * acc_sc[hh, :, lanes] + jnp.dot(vt, p, preferred_element_type=F32)
            m_sc[hh, :, lanes] = m_new

    scores(0, 0)

    def pair(jj, carry):
        j = 2 * jj
        scores(j + 1, 1)
        softmax_pv(j, 0, False)
        scores(j + 2, 0)
        softmax_pv(j + 1, 1, False)
        return carry

    lax.fori_loop(0, qi // 2, pair, 0)

    @pl.when(qi % 2 == 0)
    def _():
        softmax_pv(qi, 0, True)

    @pl.when(qi % 2 == 1)
    def _():
        scores(qi, 1)
        softmax_pv(qi - 1, 0, False)
        softmax_pv(qi, 1, True)

    lam = _diff_lambda(lq_ref[...], lam_init)
    for hh in heads:
        cols = slice(hh * V_DIM, (hh + 1) * V_DIM)
        inv = 1.0 / acc_sc[hh, V_DIM:V_DIM + 1, lanes]
        o = (acc_sc[hh, 0:V_DIM, 0:width] * inv[:, 0:width]
             - lam * (acc_sc[hh, 0:V_DIM, width:2 * width] * inv[:, width:2 * width]))
        ms = jnp.mean(o * o, axis=0, keepdims=True)
        o = o * lax.rsqrt(ms + SUBLN_EPS) * g_ref[...] * (1.0 - lam_init)
        o_ref[0, 0:width, cols] = o.T.astype(BF16)
        if width < TILE:
            o_ref[0, width:TILE, cols] = jnp.zeros((TILE - width, V_DIM), BF16)


def _prompt_attn_kernel(*refs, lam_init):
    qi = pl.program_id(2)

    @pl.when(qi < N_TILES - 1)
    def _():
        _prompt_attn_tile(TILE, *refs, lam_init=lam_init)

    @pl.when(qi == N_TILES - 1)
    def _():
        _prompt_attn_tile(LAST_W, *refs, lam_init=lam_init)


def _prompt_attn(lq_l, g_col, qx, kb, vt, layer):
    hps = HEADS_PER_STEP
    grid = (BATCH, N_HEADS // hps, N_TILES)
    kern = functools.partial(_prompt_attn_kernel, lam_init=_lam_init(layer))
    return pl.pallas_call(
        kern,
        out_shape=jax.ShapeDtypeStruct((BATCH, L_PAD, ATTN_WIDTH), BF16),
        grid=grid,
        in_specs=[
            pl.BlockSpec((4, HEAD_DIM), lambda b, h, i: (0, 0)),
            pl.BlockSpec((V_DIM, 1), lambda b, h, i: (0, 0)),
            pl.BlockSpec((1, 1, hps * 2 * HEAD_DIM, 2 * TILE), lambda b, h, i: (b, i, h, 0)),
            pl.BlockSpec((1, L_PAD, hps * 2 * HEAD_DIM), lambda b, h, i: (b, 0, h)),
            pl.BlockSpec((1, N_TILES, hps * V_EXT, TILE), lambda b, h, i: (b, 0, h, 0)),
        ],
        out_specs=pl.BlockSpec((1, TILE, hps * V_DIM), lambda b, h, i: (b, i, h)),
        scratch_shapes=[
            pltpu.VMEM((hps, TILE, 2 * TILE), F32),
            pltpu.VMEM((hps, TILE, 2 * TILE), F32),
            pltpu.VMEM((hps, 1, 2 * TILE), F32),
            pltpu.VMEM((hps, 1, 2 * TILE), F32),
            pltpu.VMEM((hps, 1, 2 * TILE), F32),
            pltpu.VMEM((hps, V_EXT, 2 * TILE), F32),
        ],
        compiler_params=pltpu.CompilerParams(
            dimension_semantics=("arbitrary", "arbitrary", "arbitrary"),
            vmem_limit_bytes=VMEM_LIMIT_BYTES),
        name="prompt_attn",
    )(lq_l, g_col, qx, kb, vt)


def _tail_rows(h_ref, a_ref, p_ref, wo_ref, wfi_ref, wfo_ref, g_ref, b_ref, o_ref):
    h = h_ref[...]
    x = jnp.concatenate([a_ref[...].astype(BF16), p_ref[...].astype(BF16)], axis=-1)
    mix = jnp.dot(x, wo_ref[...], preferred_element_type=F32)
    h1 = _layer_norm(DEEPNORM_ALPHA * h + mix, g_ref[0:1], b_ref[0:1])
    gu = jnp.dot(h1.astype(BF16), wfi_ref[...], preferred_element_type=F32)
    gate = gu[:, :D_FF]
    act = gate * jax.nn.sigmoid(gate) * gu[:, D_FF:]
    y = jnp.dot(act.astype(BF16), wfo_ref[...], preferred_element_type=F32)
    o_ref[...] = _layer_norm(DEEPNORM_ALPHA * h1 + y, g_ref[1:2], b_ref[1:2])


def _tail(h, attn, pooled, wo_all, wfi_all, wfo_all, g_l, b_l, layer):
    n_rows = h.shape[0]
    grid = (n_rows // TILE,)
    row = lambda i: (i, 0)
    const = lambda i: (0, 0)
    of_layer = lambda i: (layer, 0, 0)
    once = pl.Buffered(1)
    return pl.pallas_call(
        _tail_rows,
        out_shape=jax.ShapeDtypeStruct((n_rows, D_MODEL), F32),
        grid=grid,
        in_specs=[
            pl.BlockSpec((TILE, D_MODEL), row),
            pl.BlockSpec((TILE, ATTN_WIDTH), row),
            pl.BlockSpec((TILE, POOL_WIDTH), row),
            pl.BlockSpec((None, D_MODEL, D_MODEL), of_layer, pipeline_mode=once),
            pl.BlockSpec((None, D_MODEL, 2 * D_FF), of_layer, pipeline_mode=once),
            pl.BlockSpec((None, D_FF, D_MODEL), of_layer, pipeline_mode=once),
            pl.BlockSpec((2, D_MODEL), const),
            pl.BlockSpec((2, D_MODEL), const),
        ],
        out_specs=pl.BlockSpec((TILE, D_MODEL), row),
        compiler_params=pltpu.CompilerParams(
            dimension_semantics=("arbitrary",),
            vmem_limit_bytes=VMEM_LIMIT_BYTES),
        name="layer_tail",
    )(h, attn, pooled, wo_all, wfi_all, wfo_all, g_l, b_l)


def _proj_sample_kernel(h_ref, w_ref, wpool_ref, pscale_ref, sp_ref,
                        kf_ref, vf_ref, q_ref, pooled_ref, hist_ref, uscr, pscr):
    p = jnp.dot(h_ref[...].astype(BF16), w_ref[...], preferred_element_type=F32)
    q_ref[...] = p[:, :QK_WIDTH] * (QK_SCALE * LOG2_E)
    kf_ref[...] = p[:, QK_WIDTH:2 * QK_WIDTH]
    vf_ref[...] = p[:, 2 * QK_WIDTH:2 * QK_WIDTH + ATTN_WIDTH]
    u0 = 2 * QK_WIDTH + ATTN_WIDTH
    keep_rows = POOL_HIST - DEC_SEQ
    hist_ref[:, 0:keep_rows * POOL_WIDTH] = sp_ref[:, DEC_SEQ * POOL_WIDTH:POOL_HIST * POOL_WIDTH]
    for g, win in enumerate(POOL_WINDOWS):
        uscr[g] = p[:, u0 + g * POOL_GROUP:u0 + (g + 1) * POOL_GROUP]
        ext = [sp_ref[:, i * POOL_WIDTH + g * POOL_GROUP:i * POOL_WIDTH + (g + 1) * POOL_GROUP]
               for i in range(POOL_HIST)]
        ext += [uscr[g, pl.ds(t, DEC_BATCH, stride=DEC_SEQ), :] for t in range(DEC_SEQ)]
        for t in range(DEC_SEQ):
            r = POOL_HIST + t
            s = ext[r]
            for i in range(1, win):
                s = s + ext[r - i]
            m = s / float(win) - ext[r]
            out = jnp.dot(m.astype(BF16), wpool_ref[g], preferred_element_type=F32)
            pscr[g, pl.ds(t, DEC_BATCH, stride=DEC_SEQ), :] = (
                out * pscale_ref[:, g * POOL_GROUP:(g + 1) * POOL_GROUP])
            c0 = (keep_rows + t) * POOL_WIDTH + g * POOL_GROUP
            hist_ref[:, c0:c0 + POOL_GROUP] = ext[r]
    pooled_ref[...] = jnp.concatenate([pscr[g] for g in range(len(POOL_WINDOWS))], axis=-1)


def _proj_sample(hs, w_in_all, wpool_all, pscale_l, sp_all, layer):
    whole = lambda shape: pl.BlockSpec(shape, lambda i: (0,) * len(shape))
    in_specs = [
        whole((N_SAMPLE_ROWS, D_MODEL)),
        pl.BlockSpec((None, D_MODEL, PROJ_WIDTH), lambda i: (layer, 0, 0)),
        pl.BlockSpec((None, len(POOL_WINDOWS), POOL_GROUP, POOL_GROUP), lambda i: (layer, 0, 0, 0)),
        whole((1, POOL_WIDTH)),
        pl.BlockSpec((None, DEC_BATCH, POOL_HIST * POOL_WIDTH), lambda i: (layer, 0, 0)),
    ]
    out_shape = (
        jax.ShapeDtypeStruct((N_SAMPLE_ROWS, QK_WIDTH), F32),
        jax.ShapeDtypeStruct((N_SAMPLE_ROWS, ATTN_WIDTH), F32),
        jax.ShapeDtypeStruct((N_SAMPLE_ROWS, QK_WIDTH), F32),
        jax.ShapeDtypeStruct((N_SAMPLE_ROWS, POOL_WIDTH), F32),
        jax.ShapeDtypeStruct((DEC_BATCH, POOL_HIST * POOL_WIDTH), F32),
    )
    return pl.pallas_call(
        _proj_sample_kernel,
        out_shape=out_shape,
        grid=(1,),
        in_specs=in_specs,
        out_specs=tuple(whole(o.shape) for o in out_shape),
        scratch_shapes=[pltpu.VMEM((len(POOL_WINDOWS), N_SAMPLE_ROWS, POOL_GROUP), F32),
                        pltpu.VMEM((len(POOL_WINDOWS), N_SAMPLE_ROWS, POOL_GROUP), F32)],
        compiler_params=pltpu.CompilerParams(
            dimension_semantics=("arbitrary",),
            vmem_limit_bytes=VMEM_LIMIT_BYTES),
        name="proj_sample",
    )(hs, w_in_all, wpool_all, pscale_l, sp_all)


def _select_rows(n_out, first, repeat, src):
    n_src = src.shape[0]
    r = lax.broadcasted_iota(jnp.int32, (n_out, n_src), 0)
    c = lax.broadcasted_iota(jnp.int32, (n_out, n_src), 1)
    pick = (c == first + r % DEC_SEQ) if repeat else ((c == first + r) & (r < DEC_SEQ))
    return jnp.dot(pick.astype(F32), src, preferred_element_type=F32)


def _sample_attend(kview, vview, q_rows, k_rows, v_rows, s, lq, g_row, lam_init):
    def heads_on_lanes(view):
        parts = [view[pl.ds(h, PAST_LEN, stride=N_HEADS), :] for h in range(N_HEADS)]
        return jnp.concatenate(parts, axis=1).astype(BF16)

    n_row = 2 * N_HEADS * DEC_SEQ
    first = s * DEC_SEQ
    q_rep = _select_rows(n_row, first, True, q_rows)
    seg = lax.broadcasted_iota(jnp.int32, (n_row, QK_WIDTH), 1) // HEAD_DIM
    row = lax.broadcasted_iota(jnp.int32, (n_row, QK_WIDTH), 0)
    own = 2 * ((row // DEC_SEQ) % N_HEADS) + row // (N_HEADS * DEC_SEQ)
    qx = jnp.where(seg == own, q_rep, 0.0).astype(BF16)
    knew = _select_rows(NEW_PAD, first, False, k_rows).astype(BF16)
    vnew = _select_rows(NEW_PAD, first, False, v_rows).astype(BF16)

    s_past = lax.dot_general(qx, heads_on_lanes(kview), NT_DIMS, preferred_element_type=F32)
    s_new = lax.dot_general(qx, knew, NT_DIMS, preferred_element_type=F32)
    tt = lax.broadcasted_iota(jnp.int32, (n_row, NEW_PAD), 0) % DEC_SEQ
    jj = lax.broadcasted_iota(jnp.int32, (n_row, NEW_PAD), 1)
    s_new = jnp.where(jj <= tt, s_new, -jnp.inf)
    m = jnp.maximum(jnp.max(s_past, axis=1, keepdims=True), jnp.max(s_new, axis=1, keepdims=True))
    p_past = jnp.exp2(s_past - m)
    p_new = jnp.exp2(s_new - m)
    inv = 1.0 / (jnp.sum(p_past, axis=1, keepdims=True) + jnp.sum(p_new, axis=1, keepdims=True))
    r = jnp.dot((p_past * inv).astype(BF16), heads_on_lanes(vview), preferred_element_type=F32)
    r = r + jnp.dot((p_new * inv).astype(BF16), vnew, preferred_element_type=F32)
    lam = _diff_lambda(lq, lam_init)
    half = n_row // 2
    o = r[0:half] - lam * r[half:n_row]
    lane_h = lax.broadcasted_iota(jnp.int32, (half, ATTN_WIDTH), 1) // V_DIM
    row_h = lax.broadcasted_iota(jnp.int32, (half, ATTN_WIDTH), 0) // DEC_SEQ
    o = jnp.where(lane_h == row_h, o, 0.0)
    ms = jnp.sum(o * o, axis=1, keepdims=True) * (1.0 / V_DIM)
    o = o * lax.rsqrt(ms + SUBLN_EPS) * g_row * (1.0 - lam_init)
    out = o[0:DEC_SEQ]
    for h in range(1, N_HEADS):
        out = out + o[h * DEC_SEQ:(h + 1) * DEC_SEQ]
    return out


def _tail_sample_kernel(pt_ref, h_ref, a_ref, p_ref, wo_ref, wfi_ref, wfo_ref, g_ref, b_ref,
                        lq_ref, gs_ref, qs_ref, ks_ref, vs_ref, ck_hbm, cv_hbm,
                        o_ref, os_ref, kbuf, vbuf, sem, *, layer, lam_init):
    i = pl.program_id(0)
    n_steps = pl.num_programs(0)

    def seq_of(step, s):
        return jnp.minimum(step, DEC_BATCH // SEQ_PER_STEP - 1) * SEQ_PER_STEP + s

    def page_copies(step, s):
        seq = seq_of(step, s)
        cps = []
        for pg in range(N_PAGES):
            page = pt_ref[seq, pg]
            rows = pl.ds(pg * PAGE_ROWS, PAGE_ROWS)
            cps.append(pltpu.make_async_copy(ck_hbm.at[layer, page], kbuf.at[s, rows, :], sem.at[0, s]))
            cps.append(pltpu.make_async_copy(cv_hbm.at[layer, page], vbuf.at[s, rows, :], sem.at[1, s]))
        return cps

    @pl.when(i == 0)
    def _():
        for s in range(SEQ_PER_STEP):
            for cp in page_copies(0, s):
                cp.start()

    for s in range(SEQ_PER_STEP):
        for cp in page_copies(i, s):
            cp.wait()
    for s in range(SEQ_PER_STEP):
        os_ref[s * DEC_SEQ:(s + 1) * DEC_SEQ, :] = _sample_attend(
            kbuf.at[s], vbuf.at[s], qs_ref[...], ks_ref[...], vs_ref[...], s,
            lq_ref[...], gs_ref[...], lam_init)
    for s in range(SEQ_PER_STEP):
        for cp in page_copies(i + 1, s):
            cp.start()

    _tail_rows(h_ref, a_ref, p_ref, wo_ref, wfi_ref, wfo_ref, g_ref, b_ref, o_ref)

    @pl.when(i == n_steps - 1)
    def _():
        for s in range(SEQ_PER_STEP):
            for cp in page_copies(i + 1, s):
                cp.wait()


def _tail_sample(page_table, h, attn, pooled, wo_all, wfi_all, wfo_all, g_l, b_l,
                 lq_l, g_row, q_s, k_s, v_s, cache_k, cache_v, layer):
    n_rows = h.shape[0]
    n_steps = n_rows // TAIL_TILE
    n_seq_steps = DEC_BATCH // SEQ_PER_STEP
    assert n_steps >= n_seq_steps and DEC_BATCH % SEQ_PER_STEP == 0
    step_rows = SEQ_PER_STEP * DEC_SEQ
    kern = functools.partial(_tail_sample_kernel, layer=layer, lam_init=_lam_init(layer))
    row = lambda i, pt: (i, 0)
    const = lambda i, pt: (0, 0)
    seqs = lambda i, pt: (jnp.minimum(i, n_seq_steps - 1), 0)
    of_layer = lambda i, pt: (layer, 0, 0)
    once = pl.Buffered(1)
    grid_spec = pltpu.PrefetchScalarGridSpec(
        num_scalar_prefetch=1,
        grid=(n_steps,),
        in_specs=[
            pl.BlockSpec((TAIL_TILE, D_MODEL), row),
            pl.BlockSpec((TAIL_TILE, ATTN_WIDTH), row),
            pl.BlockSpec((TAIL_TILE, POOL_WIDTH), row),
            pl.BlockSpec((None, D_MODEL, D_MODEL), of_layer, pipeline_mode=once),
            pl.BlockSpec((None, D_MODEL, 2 * D_FF), of_layer, pipeline_mode=once),
            pl.BlockSpec((None, D_FF, D_MODEL), of_layer, pipeline_mode=once),
            pl.BlockSpec((2, D_MODEL), const),
            pl.BlockSpec((2, D_MODEL), const),
            pl.BlockSpec((4, HEAD_DIM), const),
            pl.BlockSpec((1, ATTN_WIDTH), const),
            pl.BlockSpec((step_rows, QK_WIDTH), seqs),
            pl.BlockSpec((step_rows, QK_WIDTH), seqs),
            pl.BlockSpec((step_rows, ATTN_WIDTH), seqs),
            pl.BlockSpec(memory_space=pl.ANY),
            pl.BlockSpec(memory_space=pl.ANY),
        ],
        out_specs=(
            pl.BlockSpec((TAIL_TILE, D_MODEL), row),
            pl.BlockSpec((step_rows, ATTN_WIDTH), seqs),
        ),
        scratch_shapes=[
            pltpu.VMEM((SEQ_PER_STEP, N_PAGES * PAGE_ROWS, 2 * HEAD_DIM), F32),
            pltpu.VMEM((SEQ_PER_STEP, N_PAGES * PAGE_ROWS, V_DIM), F32),
            pltpu.SemaphoreType.DMA((2, SEQ_PER_STEP)),
        ],
    )
    return pl.pallas_call(
        kern,
        out_shape=(
            jax.ShapeDtypeStruct((n_rows, D_MODEL), F32),
            jax.ShapeDtypeStruct((N_SAMPLE_ROWS, ATTN_WIDTH), F32),
        ),
        grid_spec=grid_spec,
        compiler_params=pltpu.CompilerParams(
            dimension_semantics=("arbitrary",),
            vmem_limit_bytes=VMEM_LIMIT_BYTES),
        name="tail_sample_attn",
    )(page_table, h, attn, pooled, wo_all, wfi_all, wfo_all, g_l, b_l,
      lq_l, g_row, q_s, k_s, v_s, cache_k, cache_v)


def kernel(x_prompt, x_sample, cache_k, cache_v, state_pool, page_table, meta_tokens,
           w_in, w_out, lambda_qk, subln_g, pool_w, pool_scale, w_ffn_in, w_ffn_out,
           ln_g, ln_b):
    assert x_prompt.shape == (BATCH, SEQ, D_MODEL)
    assert x_sample.shape == (DEC_BATCH, DEC_SEQ, D_MODEL)
    n_phys = cache_k.shape[1]
    ck = cache_k.reshape(DEPTH, n_phys, PAGE_ROWS, 2 * HEAD_DIM)
    cv = cache_v.reshape(DEPTH, n_phys, PAGE_ROWS, V_DIM)
    sp = state_pool.reshape(DEPTH, DEC_BATCH, POOL_HIST * POOL_WIDTH)

    w_in_b = w_in.astype(BF16)
    w_out_b = w_out.astype(BF16)
    w_ffn_in_b = w_ffn_in.astype(BF16)
    w_ffn_out_b = w_ffn_out.astype(BF16)
    pool_w_b = pool_w.astype(BF16)

    meta = jnp.broadcast_to(meta_tokens[None], (BATCH, N_META, D_MODEL))
    tail_pad = jnp.zeros((BATCH, L_PAD - L_PROMPT, D_MODEL), F32)
    hp = jnp.concatenate([meta, x_prompt, tail_pad], axis=1)
    hs = x_sample.reshape(N_SAMPLE_ROWS, D_MODEL)

    k_all = v_all = None
    u_p, k_s, v_s, u_s = [], [], [], []
    for l in range(DEPTH):
        pscale = pool_scale[l].reshape(1, POOL_WIDTH)
        g_col = subln_g[l].reshape(V_DIM, 1)
        g_row = jnp.tile(subln_g[l], N_HEADS).reshape(1, ATTN_WIDTH)

        k_all, v_all, kb, vt, qx_p, pooled, hist = _proj_prompt(
            hp, w_in_b, pool_w_b, pscale, k_all, v_all, l)
        attn = _prompt_attn(lambda_qk[l], g_col, qx_p, kb, vt, l)
        u_p.append(hist)

        kfs, vfs, qs, pooled_s, hist_s = _proj_sample(hs, w_in_b, pool_w_b, pscale, sp, l)
        hp, attn_s = _tail_sample(
            page_table, hp.reshape(BATCH * L_PAD, D_MODEL),
            attn.reshape(BATCH * L_PAD, ATTN_WIDTH), pooled.reshape(BATCH * L_PAD, POOL_WIDTH),
            w_out_b, w_ffn_in_b, w_ffn_out_b, ln_g[l], ln_b[l],
            lambda_qk[l], g_row, qs, kfs, vfs, ck, cv, l)
        hp = hp.reshape(BATCH, L_PAD, D_MODEL)
        hs = _tail(hs, attn_s, pooled_s, w_out_b, w_ffn_in_b, w_ffn_out_b, ln_g[l], ln_b[l], l)
        k_s.append(kfs)
        v_s.append(vfs)
        u_s.append(hist_s)

    def prompt_heads(x):
        return x.reshape(DEPTH, BATCH, L_PROMPT, N_HEADS, 2 * HEAD_DIM)

    def sample_heads(xs):
        return jnp.stack(xs).reshape(DEPTH, DEC_BATCH, DEC_SEQ, N_HEADS, 2 * HEAD_DIM)

    y_prompt = hp[:, N_META:L_PROMPT]
    y_sample = hs.reshape(DEC_BATCH, DEC_SEQ, D_MODEL)
    return (y_prompt, y_sample, prompt_heads(k_all), prompt_heads(v_all), jnp.stack(u_p),
            sample_heads(k_s), sample_heads(v_s),
            jnp.stack(u_s).reshape(DEPTH, DEC_BATCH, POOL_HIST, POOL_WIDTH))
```

```python
import functools
import math

import jax
import jax.numpy as jnp
from jax import lax
from jax.experimental import pallas as pl
from jax.experimental.pallas import tpu as pltpu

D_MODEL = 1024
BATCH = 2
SEQ = 8192
DEPTH = 4
DEC_BATCH = 128
DEC_SEQ = 4
PAST_LEN = 2048
PAGE_SIZE = 128
N_META = 16
ATTN_WIDTH = 512
POOL_WIDTH = 512
HEAD_DIM = 64
V_DIM = 128
N_HEADS = 4
QK_WIDTH = 512
POOL_WINDOWS = (2, 4, 8, 16)
POOL_GROUP = 128
POOL_HIST = 15
PROJ_WIDTH = 2048
D_FF = 2816
DEEPNORM_ALPHA = (2 * DEPTH) ** 0.25
LN_EPS = 1e-5
SUBLN_EPS = 1e-5
QK_SCALE = HEAD_DIM ** -0.5
LOG2_E = math.log2(math.e)

L_PROMPT = N_META + SEQ
TILE = 512
N_TILES = -(-L_PROMPT // TILE)
L_PAD = N_TILES * TILE
LAST_Q = L_PROMPT - (N_TILES - 1) * TILE
LAST_W = -(-LAST_Q // 128) * 128
V_EXT = V_DIM + 16
HEADS_PER_STEP = 4
N_PAGES = PAST_LEN // PAGE_SIZE
PAGE_ROWS = PAGE_SIZE * N_HEADS
N_SAMPLE_ROWS = DEC_BATCH * DEC_SEQ
TAIL_TILE = 272
SEQ_PER_STEP = 2
HIST_PAD = 16
NEW_PAD = 16
VMEM_LIMIT_BYTES = 56 * 1024 * 1024

F32 = jnp.float32
BF16 = jnp.bfloat16
NT_DIMS = (((1,), (1,)), ((), ()))


def _lam_init(layer):
    return 0.8 - 0.6 * math.exp(-0.3 * layer)


def _diff_lambda(lq, lam_init):
    a = jnp.sum(lq[0:1] * lq[1:2], axis=1, keepdims=True)
    b = jnp.sum(lq[2:3] * lq[3:4], axis=1, keepdims=True)
    return jnp.exp(a) - jnp.exp(b) + lam_init


def _layer_norm(x, g, b):
    mu = jnp.mean(x, axis=-1, keepdims=True)
    xc = x - mu
    var = jnp.mean(xc * xc, axis=-1, keepdims=True)
    return xc * lax.rsqrt(var + LN_EPS) * g + b


def _proj_prompt_kernel(*refs, first_layer):
    h_ref, w_ref, wpool_ref, pscale_ref = refs[:4]
    kf_ref, vf_ref, kb_ref, vt_ref, qx_ref, pooled_ref, hist_ref, ubuf = refs[4 if first_layer else 6:]
    if first_layer:
        later = jnp.zeros((DEPTH - 1, TILE * N_HEADS, V_DIM), F32)
        kf_ref[1:DEPTH, 0] = later
        vf_ref[1:DEPTH, 0] = later
    j = pl.program_id(1)
    p = jnp.dot(h_ref[0].astype(BF16), w_ref[...], preferred_element_type=F32)
    k = p[:, QK_WIDTH:2 * QK_WIDTH]
    v = p[:, 2 * QK_WIDTH:2 * QK_WIDTH + ATTN_WIDTH]
    u = p[:, 2 * QK_WIDTH + ATTN_WIDTH:]
    for h in range(N_HEADS):
        cols = slice(h * V_DIM, (h + 1) * V_DIM)
        kf_ref[0, 0, pl.ds(h, TILE, stride=N_HEADS), :] = k[:, cols]
        vf_ref[0, 0, pl.ds(h, TILE, stride=N_HEADS), :] = v[:, cols]
    kb_ref[0] = k.astype(BF16)
    vt = v.T.astype(BF16)
    ones = jnp.ones((V_EXT - V_DIM, TILE), BF16)
    for h in range(N_HEADS):
        vt_ref[0, 0, h * V_EXT:h * V_EXT + V_DIM, :] = vt[h * V_DIM:(h + 1) * V_DIM]
        vt_ref[0, 0, h * V_EXT + V_DIM:(h + 1) * V_EXT, :] = ones
    qt = (p[:, :QK_WIDTH] * (QK_SCALE * LOG2_E)).T.astype(BF16)
    zeros = jnp.zeros((HEAD_DIM, TILE), BF16)
    for h in range(N_HEADS):
        r = h * 2 * HEAD_DIM
        qx_ref[0, 0, r:r + HEAD_DIM, 0:TILE] = qt[r:r + HEAD_DIM]
        qx_ref[0, 0, r + HEAD_DIM:r + 2 * HEAD_DIM, 0:TILE] = zeros
        qx_ref[0, 0, r:r + HEAD_DIM, TILE:2 * TILE] = zeros
        qx_ref[0, 0, r + HEAD_DIM:r + 2 * HEAD_DIM, TILE:2 * TILE] = qt[r + HEAD_DIM:r + 2 * HEAD_DIM]

    @pl.when(j == 0)
    def _():
        ubuf[0:HIST_PAD, :] = jnp.zeros((HIST_PAD, POOL_WIDTH), F32)

    @pl.when(j > 0)
    def _():
        ubuf[0:HIST_PAD, :] = ubuf[TILE:TILE + HIST_PAD, :]

    ubuf[HIST_PAD:HIST_PAD + TILE, :] = u
    pos = j * TILE + lax.broadcasted_iota(jnp.int32, (TILE, POOL_GROUP), 0)
    outs = []
    for g, win in enumerate(POOL_WINDOWS):
        c0 = g * POOL_GROUP
        s = ubuf[HIST_PAD:HIST_PAD + TILE, c0:c0 + POOL_GROUP]
        for i in range(1, win):
            s = s + ubuf[HIST_PAD - i:HIST_PAD - i + TILE, c0:c0 + POOL_GROUP]
        cnt = jnp.minimum(pos + 1, win).astype(F32)
        m = s / cnt - u[:, c0:c0 + POOL_GROUP]
        outs.append(jnp.dot(m.astype(BF16), wpool_ref[g], preferred_element_type=F32))
    pooled_ref[0] = (jnp.concatenate(outs, axis=-1) * pscale_ref[...]).astype(BF16)

    @pl.when(j == N_TILES - 1)
    def _():
        r0 = HIST_PAD + (L_PROMPT - POOL_HIST) - (N_TILES - 1) * TILE
        hist_ref[0] = ubuf[r0:r0 + POOL_HIST, :]


def _proj_prompt(hp, w_in_all, wpool_all, pscale_l, k_all, v_all, layer):
    first_layer = k_all is None
    assert first_layer == (layer == 0)
    grid = (BATCH, N_TILES)
    row = lambda b, j: (b, j, 0)
    const2 = lambda b, j: (0, 0)
    kv_shape = jax.ShapeDtypeStruct((DEPTH, BATCH, L_PROMPT * N_HEADS, V_DIM), F32)
    kv_layers = DEPTH if first_layer else 1
    kv_spec = pl.BlockSpec((kv_layers, 1, TILE * N_HEADS, V_DIM), lambda b, j: (layer, b, j, 0))
    out_shape = (
        kv_shape,
        kv_shape,
        jax.ShapeDtypeStruct((BATCH, L_PAD, QK_WIDTH), BF16),
        jax.ShapeDtypeStruct((BATCH, N_TILES, N_HEADS * V_EXT, TILE), BF16),
        jax.ShapeDtypeStruct((BATCH, N_TILES, QK_WIDTH, 2 * TILE), BF16),
        jax.ShapeDtypeStruct((BATCH, L_PAD, POOL_WIDTH), BF16),
        jax.ShapeDtypeStruct((BATCH, POOL_HIST, POOL_WIDTH), F32),
    )
    out_specs = (
        kv_spec,
        kv_spec,
        pl.BlockSpec((1, TILE, QK_WIDTH), row),
        pl.BlockSpec((1, 1, N_HEADS * V_EXT, TILE), lambda b, j: (b, j, 0, 0)),
        pl.BlockSpec((1, 1, QK_WIDTH, 2 * TILE), lambda b, j: (b, j, 0, 0)),
        pl.BlockSpec((1, TILE, POOL_WIDTH), row),
        pl.BlockSpec((1, POOL_HIST, POOL_WIDTH), lambda b, j: (b, 0, 0)),
    )
    in_specs = [
        pl.BlockSpec((1, TILE, D_MODEL), row),
        pl.BlockSpec((None, D_MODEL, PROJ_WIDTH), lambda b, j: (layer, 0, 0)),
        pl.BlockSpec((None, len(POOL_WINDOWS), POOL_GROUP, POOL_GROUP), lambda b, j: (layer, 0, 0, 0)),
        pl.BlockSpec((1, POOL_WIDTH), const2),
    ]
    args = [hp, w_in_all, wpool_all, pscale_l]
    aliases = {}
    if not first_layer:
        in_specs += [pl.BlockSpec(memory_space=pl.ANY), pl.BlockSpec(memory_space=pl.ANY)]
        args += [k_all, v_all]
        aliases = {4: 0, 5: 1}
    return pl.pallas_call(
        functools.partial(_proj_prompt_kernel, first_layer=first_layer),
        out_shape=out_shape,
        grid=grid,
        in_specs=in_specs,
        out_specs=out_specs,
        scratch_shapes=[pltpu.VMEM((HIST_PAD + TILE, POOL_WIDTH), F32)],
        input_output_aliases=aliases,
        compiler_params=pltpu.CompilerParams(
            dimension_semantics=("arbitrary", "arbitrary"),
            vmem_limit_bytes=VMEM_LIMIT_BYTES),
        name="proj_prompt",
    )(*args)


def _prompt_attn_tile(width, lq_ref, g_ref, qx_ref, k_ref, vt_ref, o_ref,
                      s0_sc, s1_sc, mx0_sc, mx1_sc, m_sc, acc_sc, *, lam_init):
    qi = pl.program_id(2)
    slots = ((s0_sc, mx0_sc), (s1_sc, mx1_sc))
    lanes = slice(0, 2 * width)
    heads = range(HEADS_PER_STEP)
    qs = []
    for hh in heads:
        rows = slice(hh * 2 * HEAD_DIM, (hh + 1) * 2 * HEAD_DIM)
        if width == TILE:
            qs.append(qx_ref[0, 0, rows, :])
        else:
            qs.append(jnp.concatenate([qx_ref[0, 0, rows, 0:width],
                                       qx_ref[0, 0, rows, TILE:TILE + width]], axis=1))
        m_sc[hh, :, lanes] = jnp.full((1, 2 * width), -jnp.inf, F32)
        acc_sc[hh, :, lanes] = jnp.zeros((V_EXT, 2 * width), F32)

    def scores(j, slot):
        s_sc, mx_sc = slots[slot]
        for hh in heads:
            k = k_ref[0, pl.ds(pl.multiple_of(j * TILE, TILE), TILE), hh * 2 * HEAD_DIM:(hh + 1) * 2 * HEAD_DIM]
            s = jnp.dot(k, qs[hh], preferred_element_type=F32)
            s_sc[hh, :, lanes] = s
            mx_sc[hh, :, lanes] = jnp.max(s, axis=0, keepdims=True)

    def softmax_pv(j, slot, masked):
        s_sc, mx_sc = slots[slot]
        for hh in heads:
            s = s_sc[hh, :, lanes]
            if masked:
                kk = lax.broadcasted_iota(jnp.int32, (TILE, 2 * width), 0)
                qq = lax.broadcasted_iota(jnp.int32, (TILE, 2 * width), 1) & (width - 1)
                s = jnp.where(kk <= qq, s, -jnp.inf)
                mx = jnp.max(s, axis=0, keepdims=True)
            else:
                mx = mx_sc[hh, :, lanes]
            m_old = m_sc[hh, :, lanes]
            m_new = jnp.maximum(m_old, mx)
            a = jnp.exp2(m_old - m_new)
            p = jnp.exp2(s - m_new).astype(BF16)
            vt = vt_ref[0, j, hh * V_EXT:(hh + 1) * V_EXT, :]
            acc_sc[hh, :, lanes] = a * acc_sc[hh, :, lanes] + jnp.dot(vt, p, preferred_element_type=F32)
            m_sc[hh, :, lanes] = m_new

    scores(0, 0)

    def pair(jj, carry):
        j = 2 * jj
        scores(j + 1, 1)
        softmax_pv(j, 0, False)
        scores(j + 2, 0)
        softmax_pv(j + 1, 1, False)
        return carry

    lax.fori_loop(0, qi // 2, pair, 0)

    @pl.when(qi % 2 == 0)
    def _():
        softmax_pv(qi, 0, True)

    @pl.when(qi % 2 == 1)
    def _():
        scores(qi, 1)
        softmax_pv(qi - 1, 0, False)
        softmax_pv(qi, 1, True)

    lam = _diff_lambda(lq_ref[...], lam_init)
    for hh in heads:
        cols = slice(hh * V_DIM, (hh + 1) * V_DIM)
        inv = 1.0 / acc_sc[hh, V_DIM:V_DIM + 1, lanes]
        o = (acc_sc[hh, 0:V_DIM, 0:width] * inv[:, 0:width]
             - lam * (acc_sc[hh, 0:V_DIM, width:2 * width] * inv[:, width:2 * width]))
        ms = jnp.mean(o * o, axis=0, keepdims=True)
        o = o * lax.rsqrt(ms + SUBLN_EPS) * g_ref[...] * (1.0 - lam_init)
        o_ref[0, 0:width, cols] = o.T.astype(BF16)
        if width < TILE:
            o_ref[0, width:TILE, cols] = jnp.zeros((TILE - width, V_DIM), BF16)


def _prompt_attn_kernel(*refs, lam_init):
    qi = pl.program_id(2)

    @pl.when(qi < N_TILES - 1)
    def _():
        _prompt_attn_tile(TILE, *refs, lam_init=lam_init)

    @pl.when(qi == N_TILES - 1)
    def _():
        _prompt_attn_tile(LAST_W, *refs, lam_init=lam_init)


def _prompt_attn(lq_l, g_col, qx, kb, vt, layer):
    hps = HEADS_PER_STEP
    grid = (BATCH, N_HEADS // hps, N_TILES)
    kern = functools.partial(_prompt_attn_kernel, lam_init=_lam_init(layer))
    return pl.pallas_call(
        kern,
        out_shape=jax.ShapeDtypeStruct((BATCH, L_PAD, ATTN_WIDTH), BF16),
        grid=grid,
        in_specs=[
            pl.BlockSpec((4, HEAD_DIM), lambda b, h, i: (0, 0)),
            pl.BlockSpec((V_DIM, 1), lambda b, h, i: (0, 0)),
            pl.BlockSpec((1, 1, hps * 2 * HEAD_DIM, 2 * TILE), lambda b, h, i: (b, i, h, 0)),
            pl.BlockSpec((1, L_PAD, hps * 2 * HEAD_DIM), lambda b, h, i: (b, 0, h),
                         pipeline_mode=pl.Buffered(1)),
            pl.BlockSpec((1, N_TILES, hps * V_EXT, TILE), lambda b, h, i: (b, 0, h, 0),
                         pipeline_mode=pl.Buffered(1)),
        ],
        out_specs=pl.BlockSpec((1, TILE, hps * V_DIM), lambda b, h, i: (b, i, h)),
        scratch_shapes=[
            pltpu.VMEM((hps, TILE, 2 * TILE), F32),
            pltpu.VMEM((hps, TILE, 2 * TILE), F32),
            pltpu.VMEM((hps, 1, 2 * TILE), F32),
            pltpu.VMEM((hps, 1, 2 * TILE), F32),
            pltpu.VMEM((hps, 1, 2 * TILE), F32),
            pltpu.VMEM((hps, V_EXT, 2 * TILE), F32),
        ],
        compiler_params=pltpu.CompilerParams(
            dimension_semantics=("arbitrary", "arbitrary", "arbitrary"),
            vmem_limit_bytes=VMEM_LIMIT_BYTES),
        name="prompt_attn",
    )(lq_l, g_col, qx, kb, vt)


def _tail_rows(h_ref, a_ref, p_ref, wo_ref, wfi_ref, wfo_ref, g_ref, b_ref, o_ref):
    h = h_ref[...]
    x = jnp.concatenate([a_ref[...].astype(BF16), p_ref[...].astype(BF16)], axis=-1)
    mix = jnp.dot(x, wo_ref[...], preferred_element_type=F32)
    h1 = _layer_norm(DEEPNORM_ALPHA * h + mix, g_ref[0:1], b_ref[0:1])
    gu = jnp.dot(h1.astype(BF16), wfi_ref[...], preferred_element_type=F32)
    gate = gu[:, :D_FF]
    act = gate * jax.nn.sigmoid(gate) * gu[:, D_FF:]
    y = jnp.dot(act.astype(BF16), wfo_ref[...], preferred_element_type=F32)
    o_ref[...] = _layer_norm(DEEPNORM_ALPHA * h1 + y, g_ref[1:2], b_ref[1:2])


def _tail(h, attn, pooled, wo_all, wfi_all, wfo_all, g_l, b_l, layer):
    n_rows = h.shape[0]
    grid = (n_rows // TILE,)
    row = lambda i: (i, 0)
    const = lambda i: (0, 0)
    of_layer = lambda i: (layer, 0, 0)
    once = pl.Buffered(1)
    return pl.pallas_call(
        _tail_rows,
        out_shape=jax.ShapeDtypeStruct((n_rows, D_MODEL), F32),
        grid=grid,
        in_specs=[
            pl.BlockSpec((TILE, D_MODEL), row),
            pl.BlockSpec((TILE, ATTN_WIDTH), row),
            pl.BlockSpec((TILE, POOL_WIDTH), row),
            pl.BlockSpec((None, D_MODEL, D_MODEL), of_layer, pipeline_mode=once),
            pl.BlockSpec((None, D_MODEL, 2 * D_FF), of_layer, pipeline_mode=once),
            pl.BlockSpec((None, D_FF, D_MODEL), of_layer, pipeline_mode=once),
            pl.BlockSpec((2, D_MODEL), const),
            pl.BlockSpec((2, D_MODEL), const),
        ],
        out_specs=pl.BlockSpec((TILE, D_MODEL), row),
        compiler_params=pltpu.CompilerParams(
            dimension_semantics=("arbitrary",),
            vmem_limit_bytes=VMEM_LIMIT_BYTES),
        name="layer_tail",
    )(h, attn, pooled, wo_all, wfi_all, wfo_all, g_l, b_l)


def _proj_sample_kernel(h_ref, w_ref, wpool_ref, pscale_ref, sp_ref,
                        kf_ref, vf_ref, q_ref, pooled_ref, hist_ref, uscr, pscr):
    p = jnp.dot(h_ref[...].astype(BF16), w_ref[...], preferred_element_type=F32)
    q_ref[...] = p[:, :QK_WIDTH] * (QK_SCALE * LOG2_E)
    kf_ref[...] = p[:, QK_WIDTH:2 * QK_WIDTH]
    vf_ref[...] = p[:, 2 * QK_WIDTH:2 * QK_WIDTH + ATTN_WIDTH]
    u0 = 2 * QK_WIDTH + ATTN_WIDTH
    keep_rows = POOL_HIST - DEC_SEQ
    hist_ref[:, 0:keep_rows * POOL_WIDTH] = sp_ref[:, DEC_SEQ * POOL_WIDTH:POOL_HIST * POOL_WIDTH]
    for g, win in enumerate(POOL_WINDOWS):
        uscr[g] = p[:, u0 + g * POOL_GROUP:u0 + (g + 1) * POOL_GROUP]
        ext = [sp_ref[:, i * POOL_WIDTH + g * POOL_GROUP:i * POOL_WIDTH + (g + 1) * POOL_GROUP]
               for i in range(POOL_HIST)]
        ext += [uscr[g, pl.ds(t, DEC_BATCH, stride=DEC_SEQ), :] for t in range(DEC_SEQ)]
        for t in range(DEC_SEQ):
            r = POOL_HIST + t
            s = ext[r]
            for i in range(1, win):
                s = s + ext[r - i]
            m = s / float(win) - ext[r]
            out = jnp.dot(m.astype(BF16), wpool_ref[g], preferred_element_type=F32)
            pscr[g, pl.ds(t, DEC_BATCH, stride=DEC_SEQ), :] = (
                out * pscale_ref[:, g * POOL_GROUP:(g + 1) * POOL_GROUP])
            c0 = (keep_rows + t) * POOL_WIDTH + g * POOL_GROUP
            hist_ref[:, c0:c0 + POOL_GROUP] = ext[r]
    pooled_ref[...] = jnp.concatenate([pscr[g] for g in range(len(POOL_WINDOWS))], axis=-1)


def _proj_sample(hs, w_in_all, wpool_all, pscale_l, sp_all, layer):
    whole = lambda shape: pl.BlockSpec(shape, lambda i: (0,) * len(shape))
    in_specs = [
        whole((N_SAMPLE_ROWS, D_MODEL)),
        pl.BlockSpec((None, D_MODEL, PROJ_WIDTH), lambda i: (layer, 0, 0)),
        pl.BlockSpec((None, len(POOL_WINDOWS), POOL_GROUP, POOL_GROUP), lambda i: (layer, 0, 0, 0)),
        whole((1, POOL_WIDTH)),
        pl.BlockSpec((None, DEC_BATCH, POOL_HIST * POOL_WIDTH), lambda i: (layer, 0, 0)),
    ]
    out_shape = (
        jax.ShapeDtypeStruct((N_SAMPLE_ROWS, QK_WIDTH), F32),
        jax.ShapeDtypeStruct((N_SAMPLE_ROWS, ATTN_WIDTH), F32),
        jax.ShapeDtypeStruct((N_SAMPLE_ROWS, QK_WIDTH), F32),
        jax.ShapeDtypeStruct((N_SAMPLE_ROWS, POOL_WIDTH), F32),
        jax.ShapeDtypeStruct((DEC_BATCH, POOL_HIST * POOL_WIDTH), F32),
    )
    return pl.pallas_call(
        _proj_sample_kernel,
        out_shape=out_shape,
        grid=(1,),
        in_specs=in_specs,
        out_specs=tuple(whole(o.shape) for o in out_shape),
        scratch_shapes=[pltpu.VMEM((len(POOL_WINDOWS), N_SAMPLE_ROWS, POOL_GROUP), F32),
                        pltpu.VMEM((len(POOL_WINDOWS), N_SAMPLE_ROWS, POOL_GROUP), F32)],
        compiler_params=pltpu.CompilerParams(
            dimension_semantics=("arbitrary",),
            vmem_limit_bytes=VMEM_LIMIT_BYTES),
        name="proj_sample",
    )(hs, w_in_all, wpool_all, pscale_l, sp_all)


def _select_rows(n_out, first, repeat, src):
    n_src = src.shape[0]
    r = lax.broadcasted_iota(jnp.int32, (n_out, n_src), 0)
    c = lax.broadcasted_iota(jnp.int32, (n_out, n_src), 1)
    pick = (c == first + r % DEC_SEQ) if repeat else ((c == first + r) & (r < DEC_SEQ))
    return jnp.dot(pick.astype(F32), src, preferred_element_type=F32)


def _sample_attend(kview, vview, q_rows, k_rows, v_rows, s, lq, g_row, lam_init):
    def heads_on_lanes(view):
        parts = [view[pl.ds(h, PAST_LEN, stride=N_HEADS), :] for h in range(N_HEADS)]
        return jnp.concatenate(parts, axis=1).astype(BF16)

    n_row = 2 * N_HEADS * DEC_SEQ
    first = s * DEC_SEQ
    q_rep = _select_rows(n_row, first, True, q_rows)
    seg = lax.broadcasted_iota(jnp.int32, (n_row, QK_WIDTH), 1) // HEAD_DIM
    row = lax.broadcasted_iota(jnp.int32, (n_row, QK_WIDTH), 0)
    own = 2 * ((row // DEC_SEQ) % N_HEADS) + row // (N_HEADS * DEC_SEQ)
    qx = jnp.where(seg == own, q_rep, 0.0).astype(BF16)
    knew = _select_rows(NEW_PAD, first, False, k_rows).astype(BF16)
    vnew = _select_rows(NEW_PAD, first, False, v_rows).astype(BF16)

    s_past = lax.dot_general(qx, heads_on_lanes(kview), NT_DIMS, preferred_element_type=F32)
    s_new = lax.dot_general(qx, knew, NT_DIMS, preferred_element_type=F32)
    tt = lax.broadcasted_iota(jnp.int32, (n_row, NEW_PAD), 0) % DEC_SEQ
    jj = lax.broadcasted_iota(jnp.int32, (n_row, NEW_PAD), 1)
    s_new = jnp.where(jj <= tt, s_new, -jnp.inf)
    m = jnp.maximum(jnp.max(s_past, axis=1, keepdims=True), jnp.max(s_new, axis=1, keepdims=True))
    p_past = jnp.exp2(s_past - m)
    p_new = jnp.exp2(s_new - m)
    inv = 1.0 / (jnp.sum(p_past, axis=1, keepdims=True) + jnp.sum(p_new, axis=1, keepdims=True))
    r = jnp.dot((p_past * inv).astype(BF16), heads_on_lanes(vview), preferred_element_type=F32)
    r = r + jnp.dot((p_new * inv).astype(BF16), vnew, preferred_element_type=F32)
    lam = _diff_lambda(lq, lam_init)
    half = n_row // 2
    o = r[0:half] - lam * r[half:n_row]
    lane_h = lax.broadcasted_iota(jnp.int32, (half, ATTN_WIDTH), 1) // V_DIM
    row_h = lax.broadcasted_iota(jnp.int32, (half, ATTN_WIDTH), 0) // DEC_SEQ
    o = jnp.where(lane_h == row_h, o, 0.0)
    ms = jnp.sum(o * o, axis=1, keepdims=True) * (1.0 / V_DIM)
    o = o * lax.rsqrt(ms + SUBLN_EPS) * g_row * (1.0 - lam_init)
    out = o[0:DEC_SEQ]
    for h in range(1, N_HEADS):
        out = out + o[h * DEC_SEQ:(h + 1) * DEC_SEQ]
    return out


def _tail_sample_kernel(pt_ref, h_ref, a_ref, p_ref, wo_ref, wfi_ref, wfo_ref, g_ref, b_ref,
                        lq_ref, gs_ref, qs_ref, ks_ref, vs_ref, ck_hbm, cv_hbm,
                        o_ref, os_ref, kbuf, vbuf, sem, *, layer, lam_init):
    i = pl.program_id(0)
    n_steps = pl.num_programs(0)

    def seq_of(step, s):
        return jnp.minimum(step, DEC_BATCH // SEQ_PER_STEP - 1) * SEQ_PER_STEP + s

    def page_copies(step, s):
        seq = seq_of(step, s)
        cps = []
        for pg in range(N_PAGES):
            page = pt_ref[seq, pg]
            rows = pl.ds(pg * PAGE_ROWS, PAGE_ROWS)
            cps.append(pltpu.make_async_copy(ck_hbm.at[layer, page], kbuf.at[s, rows, :], sem.at[0, s]))
            cps.append(pltpu.make_async_copy(cv_hbm.at[layer, page], vbuf.at[s, rows, :], sem.at[1, s]))
        return cps

    @pl.when(i == 0)
    def _():
        for s in range(SEQ_PER_STEP):
            for cp in page_copies(0, s):
                cp.start()

    for s in range(SEQ_PER_STEP):
        for cp in page_copies(i, s):
            cp.wait()
    for s in range(SEQ_PER_STEP):
        os_ref[s * DEC_SEQ:(s + 1) * DEC_SEQ, :] = _sample_attend(
            kbuf.at[s], vbuf.at[s], qs_ref[...], ks_ref[...], vs_ref[...], s,
            lq_ref[...], gs_ref[...], lam_init)
    for s in range(SEQ_PER_STEP):
        for cp in page_copies(i + 1, s):
            cp.start()

    _tail_rows(h_ref, a_ref, p_ref, wo_ref, wfi_ref, wfo_ref, g_ref, b_ref, o_ref)

    @pl.when(i == n_steps - 1)
    def _():
        for s in range(SEQ_PER_STEP):
            for cp in page_copies(i + 1, s):
                cp.wait()


def _tail_sample(page_table, h, attn, pooled, wo_all, wfi_all, wfo_all, g_l, b_l,
                 lq_l, g_row, q_s, k_s, v_s, cache_k, cache_v, layer):
    n_rows = h.shape[0]
    n_steps = n_rows // TAIL_TILE
    n_seq_steps = DEC_BATCH // SEQ_PER_STEP
    assert n_steps >= n_seq_steps and DEC_BATCH % SEQ_PER_STEP == 0
    step_rows = SEQ_PER_STEP * DEC_SEQ
    kern = functools.partial(_tail_sample_kernel, layer=layer, lam_init=_lam_init(layer))
    row = lambda i, pt: (i, 0)
    const = lambda i, pt: (0, 0)
    seqs = lambda i, pt: (jnp.minimum(i, n_seq_steps - 1), 0)
    of_layer = lambda i, pt: (layer, 0, 0)
    once = pl.Buffered(1)
    grid_spec = pltpu.PrefetchScalarGridSpec(
        num_scalar_prefetch=1,
        grid=(n_steps,),
        in_specs=[
            pl.BlockSpec((TAIL_TILE, D_MODEL), row),
            pl.BlockSpec((TAIL_TILE, ATTN_WIDTH), row),
            pl.BlockSpec((TAIL_TILE, POOL_WIDTH), row),
            pl.BlockSpec((None, D_MODEL, D_MODEL), of_layer, pipeline_mode=once),
            pl.BlockSpec((None, D_MODEL, 2 * D_FF), of_layer, pipeline_mode=once),
            pl.BlockSpec((None, D_FF, D_MODEL), of_layer, pipeline_mode=once),
            pl.BlockSpec((2, D_MODEL), const),
            pl.BlockSpec((2, D_MODEL), const),
            pl.BlockSpec((4, HEAD_DIM), const),
            pl.BlockSpec((1, ATTN_WIDTH), const),
            pl.BlockSpec((step_rows, QK_WIDTH), seqs),
            pl.BlockSpec((step_rows, QK_WIDTH), seqs),
            pl.BlockSpec((step_rows, ATTN_WIDTH), seqs),
            pl.BlockSpec(memory_space=pl.ANY),
            pl.BlockSpec(memory_space=pl.ANY),
        ],
        out_specs=(
            pl.BlockSpec((TAIL_TILE, D_MODEL), row),
            pl.BlockSpec((step_rows, ATTN_WIDTH), seqs),
        ),
        scratch_shapes=[
            pltpu.VMEM((SEQ_PER_STEP, N_PAGES * PAGE_ROWS, 2 * HEAD_DIM), F32),
            pltpu.VMEM((SEQ_PER_STEP, N_PAGES * PAGE_ROWS, V_DIM), F32),
            pltpu.SemaphoreType.DMA((2, SEQ_PER_STEP)),
        ],
    )
    return pl.pallas_call(
        kern,
        out_shape=(
            jax.ShapeDtypeStruct((n_rows, D_MODEL), F32),
            jax.ShapeDtypeStruct((N_SAMPLE_ROWS, ATTN_WIDTH), F32),
        ),
        grid_spec=grid_spec,
        compiler_params=pltpu.CompilerParams(
            dimension_semantics=("arbitrary",),
            vmem_limit_bytes=VMEM_LIMIT_BYTES),
        name="tail_sample_attn",
    )(page_table, h, attn, pooled, wo_all, wfi_all, wfo_all, g_l, b_l,
      lq_l, g_row, q_s, k_s, v_s, cache_k, cache_v)


def kernel(x_prompt, x_sample, cache_k, cache_v, state_pool, page_table, meta_tokens,
           w_in, w_out, lambda_qk, subln_g, pool_w, pool_scale, w_ffn_in, w_ffn_out,
           ln_g, ln_b):
    assert x_prompt.shape == (BATCH, SEQ, D_MODEL)
    assert x_sample.shape == (DEC_BATCH, DEC_SEQ, D_MODEL)
    n_phys = cache_k.shape[1]
    ck = cache_k.reshape(DEPTH, n_phys, PAGE_ROWS, 2 * HEAD_DIM)
    cv = cache_v.reshape(DEPTH, n_phys, PAGE_ROWS, V_DIM)
    sp = state_pool.reshape(DEPTH, DEC_BATCH, POOL_HIST * POOL_WIDTH)

    w_in_b = w_in.astype(BF16)
    w_out_b = w_out.astype(BF16)
    w_ffn_in_b = w_ffn_in.astype(BF16)
    w_ffn_out_b = w_ffn_out.astype(BF16)
    pool_w_b = pool_w.astype(BF16)

    meta = jnp.broadcast_to(meta_tokens[None], (BATCH, N_META, D_MODEL))
    tail_pad = jnp.zeros((BATCH, L_PAD - L_PROMPT, D_MODEL), F32)
    hp = jnp.concatenate([meta, x_prompt, tail_pad], axis=1)
    hs = x_sample.reshape(N_SAMPLE_ROWS, D_MODEL)

    k_all = v_all = None
    u_p, k_s, v_s, u_s = [], [], [], []
    for l in range(DEPTH):
        pscale = pool_scale[l].reshape(1, POOL_WIDTH)
        g_col = subln_g[l].reshape(V_DIM, 1)
        g_row = jnp.tile(subln_g[l], N_HEADS).reshape(1, ATTN_WIDTH)

        k_all, v_all, kb, vt, qx_p, pooled, hist = _proj_prompt(
            hp, w_in_b, pool_w_b, pscale, k_all, v_all, l)
        attn = _prompt_attn(lambda_qk[l], g_col, qx_p, kb, vt, l)
        u_p.append(hist)

        kfs, vfs, qs, pooled_s, hist_s = _proj_sample(hs, w_in_b, pool_w_b, pscale, sp, l)
        hp, attn_s = _tail_sample(
            page_table, hp.reshape(BATCH * L_PAD, D_MODEL),
            attn.reshape(BATCH * L_PAD, ATTN_WIDTH), pooled.reshape(BATCH * L_PAD, POOL_WIDTH),
            w_out_b, w_ffn_in_b, w_ffn_out_b, ln_g[l], ln_b[l],
            lambda_qk[l], g_row, qs, kfs, vfs, ck, cv, l)
        hp = hp.reshape(BATCH, L_PAD, D_MODEL)
        hs = _tail(hs, attn_s, pooled_s, w_out_b, w_ffn_in_b, w_ffn_out_b, ln_g[l], ln_b[l], l)
        k_s.append(kfs)
        v_s.append(vfs)
        u_s.append(hist_s)

    def prompt_heads(x):
        return x.reshape(DEPTH, BATCH, L_PROMPT, N_HEADS, 2 * HEAD_DIM)

    def sample_heads(xs):
        return jnp.stack(xs).reshape(DEPTH, DEC_BATCH, DEC_SEQ, N_HEADS, 2 * HEAD_DIM)

    y_prompt = hp[:, N_META:L_PROMPT]
    y_sample = hs.reshape(DEC_BATCH, DEC_SEQ, D_MODEL)
    return (y_prompt, y_sample, prompt_heads(k_all), prompt_heads(v_all), jnp.stack(u_p),
            sample_heads(k_s), sample_heads(v_s),
            jnp.stack(u_s).reshape(DEPTH, DEC_BATCH, POOL_HIST, POOL_WIDTH))
```

```python
import functools
import math

import jax
import jax.numpy as jnp
from jax import lax
from jax.experimental import pallas as pl
from jax.experimental.pallas import tpu as pltpu

D_MODEL = 1024
BATCH = 2
SEQ = 8192
DEPTH = 4
DEC_BATCH = 128
DEC_SEQ = 4
PAST_LEN = 2048
PAGE_SIZE = 128
N_META = 16
ATTN_WIDTH = 512
POOL_WIDTH = 512
HEAD_DIM = 64
V_DIM = 128
N_HEADS = 4
QK_WIDTH = 512
POOL_WINDOWS = (2, 4, 8, 16)
POOL_GROUP = 128
POOL_HIST = 15
PROJ_WIDTH = 2048
D_FF = 2816
DEEPNORM_ALPHA = (2 * DEPTH) ** 0.25
LN_EPS = 1e-5
SUBLN_EPS = 1e-5
QK_SCALE = HEAD_DIM ** -0.5
LOG2_E = math.log2(math.e)

L_PROMPT = N_META + SEQ
TILE = 512
N_TILES = -(-L_PROMPT // TILE)
L_PAD = N_TILES * TILE
LAST_Q = L_PROMPT - (N_TILES - 1) * TILE
LAST_W = -(-LAST_Q // 128) * 128
V_EXT = V_DIM + 16
HEADS_PER_STEP = 4
N_PAGES = PAST_LEN // PAGE_SIZE
PAGE_ROWS = PAGE_SIZE * N_HEADS
N_SAMPLE_ROWS = DEC_BATCH * DEC_SEQ
TAIL_TILE = 272
SEQ_PER_STEP = 2
SUBLANES = 8
HIST_PAD = 24
NEW_PAD = 16
VMEM_LIMIT_BYTES = 56 * 1024 * 1024

F32 = jnp.float32
BF16 = jnp.bfloat16
NT_DIMS = (((1,), (1,)), ((), ()))


def _lam_init(layer):
    return 0.8 - 0.6 * math.exp(-0.3 * layer)


def _diff_lambda(lq, lam_init):
    a = jnp.sum(lq[0:1] * lq[1:2], axis=1, keepdims=True)
    b = jnp.sum(lq[2:3] * lq[3:4], axis=1, keepdims=True)
    return jnp.exp(a) - jnp.exp(b) + lam_init


def _layer_norm(x, g, b):
    mu = jnp.mean(x, axis=-1, keepdims=True)
    xc = x - mu
    var = jnp.mean(xc * xc, axis=-1, keepdims=True)
    return xc * lax.rsqrt(var + LN_EPS) * g + b


def _proj_prompt_kernel(*refs, first_layer):
    h_ref, w_ref, wpool_ref, pscale_ref = refs[:4]
    kf_ref, vf_ref, kb_ref, vt_ref, qx_ref, pooled_ref, hist_ref, ubuf, sbuf = refs[4 if first_layer else 6:]
    if first_layer:
        later = jnp.zeros((DEPTH - 1, TILE * N_HEADS, V_DIM), F32)
        kf_ref[1:DEPTH, 0] = later
        vf_ref[1:DEPTH, 0] = later
    j = pl.program_id(1)
    p = jnp.dot(h_ref[0].astype(BF16), w_ref[...], preferred_element_type=F32)
    k = p[:, QK_WIDTH:2 * QK_WIDTH]
    v = p[:, 2 * QK_WIDTH:2 * QK_WIDTH + ATTN_WIDTH]
    u = p[:, 2 * QK_WIDTH + ATTN_WIDTH:]
    for h in range(N_HEADS):
        cols = slice(h * V_DIM, (h + 1) * V_DIM)
        kf_ref[0, 0, pl.ds(h, TILE, stride=N_HEADS), :] = k[:, cols]
        vf_ref[0, 0, pl.ds(h, TILE, stride=N_HEADS), :] = v[:, cols]
    kb_ref[0] = k.astype(BF16)
    vt = v.T.astype(BF16)
    ones = jnp.ones((V_EXT - V_DIM, TILE), BF16)
    for h in range(N_HEADS):
        vt_ref[0, 0, h * V_EXT:h * V_EXT + V_DIM, :] = vt[h * V_DIM:(h + 1) * V_DIM]
        vt_ref[0, 0, h * V_EXT + V_DIM:(h + 1) * V_EXT, :] = ones
    qt = (p[:, :QK_WIDTH] * (QK_SCALE * LOG2_E)).T.astype(BF16)
    zeros = jnp.zeros((HEAD_DIM, TILE), BF16)
    for h in range(N_HEADS):
        r = h * 2 * HEAD_DIM
        qx_ref[0, 0, r:r + HEAD_DIM, 0:TILE] = qt[r:r + HEAD_DIM]
        qx_ref[0, 0, r + HEAD_DIM:r + 2 * HEAD_DIM, 0:TILE] = zeros
        qx_ref[0, 0, r:r + HEAD_DIM, TILE:2 * TILE] = zeros
        qx_ref[0, 0, r + HEAD_DIM:r + 2 * HEAD_DIM, TILE:2 * TILE] = qt[r + HEAD_DIM:r + 2 * HEAD_DIM]

    @pl.when(j == 0)
    def _():
        ubuf[0:HIST_PAD, :] = jnp.zeros((HIST_PAD, POOL_WIDTH), F32)

    @pl.when(j > 0)
    def _():
        ubuf[0:HIST_PAD, :] = ubuf[TILE:TILE + HIST_PAD, :]

    ubuf[HIST_PAD:HIST_PAD + TILE, :] = u
    pos = j * TILE + lax.broadcasted_iota(jnp.int32, (TILE, POOL_GROUP), 0)
    lo, hi = SUBLANES, HIST_PAD + TILE
    sbuf[0:lo, :] = jnp.zeros((lo, POOL_GROUP), F32)
    outs = []
    for g, win in enumerate(POOL_WINDOWS):
        c0 = g * POOL_GROUP
        s = ubuf[lo:hi, c0:c0 + POOL_GROUP] + ubuf[lo - 1:hi - 1, c0:c0 + POOL_GROUP]
        w = 2
        while w < win:
            sbuf[lo:hi, :] = s
            s = s + sbuf[lo - w:hi - w, :]
            w *= 2
        s = s[HIST_PAD - lo:, :]
        cnt = jnp.minimum(pos + 1, win).astype(F32)
        m = s / cnt - u[:, c0:c0 + POOL_GROUP]
        outs.append(jnp.dot(m.astype(BF16), wpool_ref[g], preferred_element_type=F32))
    pooled_ref[0] = (jnp.concatenate(outs, axis=-1) * pscale_ref[...]).astype(BF16)

    @pl.when(j == N_TILES - 1)
    def _():
        r0 = HIST_PAD + (L_PROMPT - POOL_HIST) - (N_TILES - 1) * TILE
        hist_ref[0] = ubuf[r0:r0 + POOL_HIST, :]


def _proj_prompt(hp, w_in_all, wpool_all, pscale_l, k_all, v_all, layer):
    first_layer = k_all is None
    assert first_layer == (layer == 0)
    grid = (BATCH, N_TILES)
    row = lambda b, j: (b, j, 0)
    const2 = lambda b, j: (0, 0)
    kv_shape = jax.ShapeDtypeStruct((DEPTH, BATCH, L_PROMPT * N_HEADS, V_DIM), F32)
    kv_layers = DEPTH if first_layer else 1
    kv_spec = pl.BlockSpec((kv_layers, 1, TILE * N_HEADS, V_DIM), lambda b, j: (layer, b, j, 0))
    out_shape = (
        kv_shape,
        kv_shape,
        jax.ShapeDtypeStruct((BATCH, L_PAD, QK_WIDTH), BF16),
        jax.ShapeDtypeStruct((BATCH, N_TILES, N_HEADS * V_EXT, TILE), BF16),
        jax.ShapeDtypeStruct((BATCH, N_TILES, QK_WIDTH, 2 * TILE), BF16),
        jax.ShapeDtypeStruct((BATCH, L_PAD, POOL_WIDTH), BF16),
        jax.ShapeDtypeStruct((BATCH, POOL_HIST, POOL_WIDTH), F32),
    )
    out_specs = (
        kv_spec,
        kv_spec,
        pl.BlockSpec((1, TILE, QK_WIDTH), row),
        pl.BlockSpec((1, 1, N_HEADS * V_EXT, TILE), lambda b, j: (b, j, 0, 0)),
        pl.BlockSpec((1, 1, QK_WIDTH, 2 * TILE), lambda b, j: (b, j, 0, 0)),
        pl.BlockSpec((1, TILE, POOL_WIDTH), row),
        pl.BlockSpec((1, POOL_HIST, POOL_WIDTH), lambda b, j: (b, 0, 0)),
    )
    in_specs = [
        pl.BlockSpec((1, TILE, D_MODEL), row),
        pl.BlockSpec((None, D_MODEL, PROJ_WIDTH), lambda b, j: (layer, 0, 0)),
        pl.BlockSpec((None, len(POOL_WINDOWS), POOL_GROUP, POOL_GROUP), lambda b, j: (layer, 0, 0, 0)),
        pl.BlockSpec((1, POOL_WIDTH), const2),
    ]
    args = [hp, w_in_all, wpool_all, pscale_l]
    aliases = {}
    if not first_layer:
        in_specs += [pl.BlockSpec(memory_space=pl.ANY), pl.BlockSpec(memory_space=pl.ANY)]
        args += [k_all, v_all]
        aliases = {4: 0, 5: 1}
    return pl.pallas_call(
        functools.partial(_proj_prompt_kernel, first_layer=first_layer),
        out_shape=out_shape,
        grid=grid,
        in_specs=in_specs,
        out_specs=out_specs,
        scratch_shapes=[pltpu.VMEM((HIST_PAD + TILE, POOL_WIDTH), F32),
                        pltpu.VMEM((HIST_PAD + TILE, POOL_GROUP), F32)],
        input_output_aliases=aliases,
        compiler_params=pltpu.CompilerParams(
            dimension_semantics=("arbitrary", "arbitrary"),
            vmem_limit_bytes=VMEM_LIMIT_BYTES),
        name="proj_prompt",
    )(*args)


def _prompt_attn_tile(width, lq_ref, g_ref, qx_ref, k_ref, vt_ref, o_ref,
                      s0_sc, s1_sc, mx0_sc, mx1_sc, m_sc, acc_sc, *, lam_init):
    qi = pl.program_id(2)
    slots = ((s0_sc, mx0_sc), (s1_sc, mx1_sc))
    lanes = slice(0, 2 * width)
    heads = range(HEADS_PER_STEP)
    qs = []
    for hh in heads:
        rows = slice(hh * 2 * HEAD_DIM, (hh + 1) * 2 * HEAD_DIM)
        if width == TILE:
            qs.append(qx_ref[0, 0, rows, :])
        else:
            qs.append(jnp.concatenate([qx_ref[0, 0, rows, 0:width],
                                       qx_ref[0, 0, rows, TILE:TILE + width]], axis=1))
        m_sc[hh, :, lanes] = jnp.full((1, 2 * width), -jnp.inf, F32)
        acc_sc[hh, :, lanes] = jnp.zeros((V_EXT, 2 * width), F32)

    def scores(j, slot):
        s_sc, mx_sc = slots[slot]
        for hh in heads:
            k = k_ref[0, pl.ds(pl.multiple_of(j * TILE, TILE), TILE), hh * 2 * HEAD_DIM:(hh + 1) * 2 * HEAD_DIM]
            s = jnp.dot(k, qs[hh], preferred_element_type=F32)
            s_sc[hh, :, lanes] = s
            mx_sc[hh, :, lanes] = jnp.max(s, axis=0, keepdims=True)

    def softmax_pv(j, slot, masked):
        s_sc, mx_sc = slots[slot]
        for hh in heads:
            s = s_sc[hh, :, lanes]
            if masked:
                kk = lax.broadcasted_iota(jnp.int32, (TILE, 2 * width), 0)
                qq = lax.broadcasted_iota(jnp.int32, (TILE, 2 * width), 1) & (width - 1)
                s = jnp.where(kk <= qq, s, -jnp.inf)
                mx = jnp.max(s, axis=0, keepdims=True)
            else:
                mx = mx_sc[hh, :, lanes]
            m_old = m_sc[hh, :, lanes]
            m_new = jnp.maximum(m_old, mx)
            a = jnp.exp2(m_old - m_new)
            p = jnp.exp2(s - m_new).astype(BF16)
            vt = vt_ref[0, j, hh * V_EXT:(hh + 1) * V_EXT, :]
            acc_sc[hh, :, lanes] = a * acc_sc[hh, :, lanes] + jnp.dot(vt, p, preferred_element_type=F32)
            m_sc[hh, :, lanes] = m_new

    scores(0, 0)

    def pair(jj, carry):
        j = 2 * jj
        scores(j + 1, 1)
        softmax_pv(j, 0, False)
        scores(j + 2, 0)
        softmax_pv(j + 1, 1, False)
        return carry

    lax.fori_loop(0, qi // 2, pair, 0)

    @pl.when(qi % 2 == 0)
    def _():
        softmax_pv(qi, 0, True)

    @pl.when(qi % 2 == 1)
    def _():
        scores(qi, 1)
        softmax_pv(qi - 1, 0, False)
        softmax_pv(qi, 1, True)

    lam = _diff_lambda(lq_ref[...], lam_init)
    for hh in heads:
        cols = slice(hh * V_DIM, (hh + 1) * V_DIM)
        inv = 1.0 / acc_sc[hh, V_DIM:V_DIM + 1, lanes]
        o = (acc_sc[hh, 0:V_DIM, 0:width] * inv[:, 0:width]
             - lam * (acc_sc[hh, 0:V_DIM, width:2 * width] * inv[:, width:2 * width]))
        ms = jnp.mean(o * o, axis=0, keepdims=True)
        o = o * lax.rsqrt(ms + SUBLN_EPS) * g_ref[...] * (1.0 - lam_init)
        o_ref[0, 0:width, cols] = o.T.astype(BF16)
        if width < TILE:
            o_ref[0, width:TILE, cols] = jnp.zeros((TILE - width, V_DIM), BF16)


def _prompt_attn_kernel(*refs, lam_init):
    qi = pl.program_id(2)

    @pl.when(qi < N_TILES - 1)
    def _():
        _prompt_attn_tile(TILE, *refs, lam_init=lam_init)

    @pl.when(qi == N_TILES - 1)
    def _():
        _prompt_attn_tile(LAST_W, *refs, lam_init=lam_init)


def _prompt_attn(lq_l, g_col, qx, kb, vt, layer):
    hps = HEADS_PER_STEP
    grid = (BATCH, N_HEADS // hps, N_TILES)
    kern = functools.partial(_prompt_attn_kernel, lam_init=_lam_init(layer))
    return pl.pallas_call(
        kern,
        out_shape=jax.ShapeDtypeStruct((BATCH, L_PAD, ATTN_WIDTH), BF16),
        grid=grid,
        in_specs=[
            pl.BlockSpec((4, HEAD_DIM), lambda b, h, i: (0, 0)),
            pl.BlockSpec((V_DIM, 1), lambda b, h, i: (0, 0)),
            pl.BlockSpec((1, 1, hps * 2 * HEAD_DIM, 2 * TILE), lambda b, h, i: (b, i, h, 0)),
            pl.BlockSpec((1, L_PAD, hps * 2 * HEAD_DIM), lambda b, h, i: (b, 0, h),
                         pipeline_mode=pl.Buffered(1)),
            pl.BlockSpec((1, N_TILES, hps * V_EXT, TILE), lambda b, h, i: (b, 0, h, 0),
                         pipeline_mode=pl.Buffered(1)),
        ],
        out_specs=pl.BlockSpec((1, TILE, hps * V_DIM), lambda b, h, i: (b, i, h)),
        scratch_shapes=[
            pltpu.VMEM((hps, TILE, 2 * TILE), F32),
            pltpu.VMEM((hps, TILE, 2 * TILE), F32),
            pltpu.VMEM((hps, 1, 2 * TILE), F32),
            pltpu.VMEM((hps, 1, 2 * TILE), F32),
            pltpu.VMEM((hps, 1, 2 * TILE), F32),
            pltpu.VMEM((hps, V_EXT, 2 * TILE), F32),
        ],
        compiler_params=pltpu.CompilerParams(
            dimension_semantics=("arbitrary", "arbitrary", "arbitrary"),
            vmem_limit_bytes=VMEM_LIMIT_BYTES),
        name="prompt_attn",
    )(lq_l, g_col, qx, kb, vt)


def _tail_rows(h_ref, a_ref, p_ref, wo_ref, wfi_ref, wfo_ref, g_ref, b_ref, o_ref):
    h = h_ref[...]
    x = jnp.concatenate([a_ref[...].astype(BF16), p_ref[...].astype(BF16)], axis=-1)
    mix = jnp.dot(x, wo_ref[...], preferred_element_type=F32)
    h1 = _layer_norm(DEEPNORM_ALPHA * h + mix, g_ref[0:1], b_ref[0:1])
    gu = jnp.dot(h1.astype(BF16), wfi_ref[...], preferred_element_type=F32)
    gate = gu[:, :D_FF]
    act = gate * jax.nn.sigmoid(gate) * gu[:, D_FF:]
    y = jnp.dot(act.astype(BF16), wfo_ref[...], preferred_element_type=F32)
    o_ref[...] = _layer_norm(DEEPNORM_ALPHA * h1 + y, g_ref[1:2], b_ref[1:2])


def _tail(h, attn, pooled, wo_all, wfi_all, wfo_all, g_l, b_l, layer):
    n_rows = h.shape[0]
    grid = (n_rows // TILE,)
    row = lambda i: (i, 0)
    const = lambda i: (0, 0)
    of_layer = lambda i: (layer, 0, 0)
    once = pl.Buffered(1)
    return pl.pallas_call(
        _tail_rows,
        out_shape=jax.ShapeDtypeStruct((n_rows, D_MODEL), F32),
        grid=grid,
        in_specs=[
            pl.BlockSpec((TILE, D_MODEL), row),
            pl.BlockSpec((TILE, ATTN_WIDTH), row),
            pl.BlockSpec((TILE, POOL_WIDTH), row),
            pl.BlockSpec((None, D_MODEL, D_MODEL), of_layer, pipeline_mode=once),
            pl.BlockSpec((None, D_MODEL, 2 * D_FF), of_layer, pipeline_mode=once),
            pl.BlockSpec((None, D_FF, D_MODEL), of_layer, pipeline_mode=once),
            pl.BlockSpec((2, D_MODEL), const),
            pl.BlockSpec((2, D_MODEL), const),
        ],
        out_specs=pl.BlockSpec((TILE, D_MODEL), row),
        compiler_params=pltpu.CompilerParams(
            dimension_semantics=("arbitrary",),
            vmem_limit_bytes=VMEM_LIMIT_BYTES),
        name="layer_tail",
    )(h, attn, pooled, wo_all, wfi_all, wfo_all, g_l, b_l)


def _proj_sample_kernel(h_ref, w_ref, wpool_ref, pscale_ref, sp_ref,
                        kf_ref, vf_ref, q_ref, pooled_ref, hist_ref, uscr, pscr):
    p = jnp.dot(h_ref[...].astype(BF16), w_ref[...], preferred_element_type=F32)
    q_ref[...] = p[:, :QK_WIDTH] * (QK_SCALE * LOG2_E)
    kf_ref[...] = p[:, QK_WIDTH:2 * QK_WIDTH]
    vf_ref[...] = p[:, 2 * QK_WIDTH:2 * QK_WIDTH + ATTN_WIDTH]
    u0 = 2 * QK_WIDTH + ATTN_WIDTH
    keep_rows = POOL_HIST - DEC_SEQ
    hist_ref[:, 0:keep_rows * POOL_WIDTH] = sp_ref[:, DEC_SEQ * POOL_WIDTH:POOL_HIST * POOL_WIDTH]
    for g, win in enumerate(POOL_WINDOWS):
        uscr[g] = p[:, u0 + g * POOL_GROUP:u0 + (g + 1) * POOL_GROUP]
        ext = [sp_ref[:, i * POOL_WIDTH + g * POOL_GROUP:i * POOL_WIDTH + (g + 1) * POOL_GROUP]
               for i in range(POOL_HIST)]
        ext += [uscr[g, pl.ds(t, DEC_BATCH, stride=DEC_SEQ), :] for t in range(DEC_SEQ)]
        for t in range(DEC_SEQ):
            r = POOL_HIST + t
            s = ext[r]
            for i in range(1, win):
                s = s + ext[r - i]
            m = s / float(win) - ext[r]
            out = jnp.dot(m.astype(BF16), wpool_ref[g], preferred_element_type=F32)
            pscr[g, pl.ds(t, DEC_BATCH, stride=DEC_SEQ), :] = (
                out * pscale_ref[:, g * POOL_GROUP:(g + 1) * POOL_GROUP])
            c0 = (keep_rows + t) * POOL_WIDTH + g * POOL_GROUP
            hist_ref[:, c0:c0 + POOL_GROUP] = ext[r]
    pooled_ref[...] = jnp.concatenate([pscr[g] for g in range(len(POOL_WINDOWS))], axis=-1)


def _proj_sample(hs, w_in_all, wpool_all, pscale_l, sp_all, layer):
    whole = lambda shape: pl.BlockSpec(shape, lambda i: (0,) * len(shape))
    in_specs = [
        whole((N_SAMPLE_ROWS, D_MODEL)),
        pl.BlockSpec((None, D_MODEL, PROJ_WIDTH), lambda i: (layer, 0, 0)),
        pl.BlockSpec((None, len(POOL_WINDOWS), POOL_GROUP, POOL_GROUP), lambda i: (layer, 0, 0, 0)),
        whole((1, POOL_WIDTH)),
        pl.BlockSpec((None, DEC_BATCH, POOL_HIST * POOL_WIDTH), lambda i: (layer, 0, 0)),
    ]
    out_shape = (
        jax.ShapeDtypeStruct((N_SAMPLE_ROWS, QK_WIDTH), F32),
        jax.ShapeDtypeStruct((N_SAMPLE_ROWS, ATTN_WIDTH), F32),
        jax.ShapeDtypeStruct((N_SAMPLE_ROWS, QK_WIDTH), F32),
        jax.ShapeDtypeStruct((N_SAMPLE_ROWS, POOL_WIDTH), F32),
        jax.ShapeDtypeStruct((DEC_BATCH, POOL_HIST * POOL_WIDTH), F32),
    )
    return pl.pallas_call(
        _proj_sample_kernel,
        out_shape=out_shape,
        grid=(1,),
        in_specs=in_specs,
        out_specs=tuple(whole(o.shape) for o in out_shape),
        scratch_shapes=[pltpu.VMEM((len(POOL_WINDOWS), N_SAMPLE_ROWS, POOL_GROUP), F32),
                        pltpu.VMEM((len(POOL_WINDOWS), N_SAMPLE_ROWS, POOL_GROUP), F32)],
        compiler_params=pltpu.CompilerParams(
            dimension_semantics=("arbitrary",),
            vmem_limit_bytes=VMEM_LIMIT_BYTES),
        name="proj_sample",
    )(hs, w_in_all, wpool_all, pscale_l, sp_all)


def _select_rows(n_out, first, repeat, src):
    n_src = src.shape[0]
    r = lax.broadcasted_iota(jnp.int32, (n_out, n_src), 0)
    c = lax.broadcasted_iota(jnp.int32, (n_out, n_src), 1)
    pick = (c == first + r % DEC_SEQ) if repeat else ((c == first + r) & (r < DEC_SEQ))
    return jnp.dot(pick.astype(F32), src, preferred_element_type=F32)


def _sample_attend(kview, vview, q_rows, k_rows, v_rows, s, lq, g_row, lam_init):
    def heads_on_lanes(view):
        parts = [view[pl.ds(h, PAST_LEN, stride=N_HEADS), :] for h in range(N_HEADS)]
        return jnp.concatenate(parts, axis=1).astype(BF16)

    n_row = 2 * N_HEADS * DEC_SEQ
    first = s * DEC_SEQ
    q_rep = _select_rows(n_row, first, True, q_rows)
    seg = lax.broadcasted_iota(jnp.int32, (n_row, QK_WIDTH), 1) // HEAD_DIM
    row = lax.broadcasted_iota(jnp.int32, (n_row, QK_WIDTH), 0)
    own = 2 * ((row // DEC_SEQ) % N_HEADS) + row // (N_HEADS * DEC_SEQ)
    qx = jnp.where(seg == own, q_rep, 0.0).astype(BF16)
    knew = _select_rows(NEW_PAD, first, False, k_rows).astype(BF16)
    vnew = _select_rows(NEW_PAD, first, False, v_rows).astype(BF16)

    s_past = lax.dot_general(qx, heads_on_lanes(kview), NT_DIMS, preferred_element_type=F32)
    s_new = lax.dot_general(qx, knew, NT_DIMS, preferred_element_type=F32)
    tt = lax.broadcasted_iota(jnp.int32, (n_row, NEW_PAD), 0) % DEC_SEQ
    jj = lax.broadcasted_iota(jnp.int32, (n_row, NEW_PAD), 1)
    s_new = jnp.where(jj <= tt, s_new, -jnp.inf)
    m = jnp.maximum(jnp.max(s_past, axis=1, keepdims=True), jnp.max(s_new, axis=1, keepdims=True))
    p_past = jnp.exp2(s_past - m)
    p_new = jnp.exp2(s_new - m)
    inv = 1.0 / (jnp.sum(p_past, axis=1, keepdims=True) + jnp.sum(p_new, axis=1, keepdims=True))
    r = jnp.dot((p_past * inv).astype(BF16), heads_on_lanes(vview), preferred_element_type=F32)
    r = r + jnp.dot((p_new * inv).astype(BF16), vnew, preferred_element_type=F32)
    lam = _diff_lambda(lq, lam_init)
    half = n_row // 2
    o = r[0:half] - lam * r[half:n_row]
    lane_h = lax.broadcasted_iota(jnp.int32, (half, ATTN_WIDTH), 1) // V_DIM
    row_h = lax.broadcasted_iota(jnp.int32, (half, ATTN_WIDTH), 0) // DEC_SEQ
    o = jnp.where(lane_h == row_h, o, 0.0)
    ms = jnp.sum(o * o, axis=1, keepdims=True) * (1.0 / V_DIM)
    o = o * lax.rsqrt(ms + SUBLN_EPS) * g_row * (1.0 - lam_init)
    out = o[0:DEC_SEQ]
    for h in range(1, N_HEADS):
        out = out + o[h * DEC_SEQ:(h + 1) * DEC_SEQ]
    return out


def _tail_sample_kernel(pt_ref, h_ref, a_ref, p_ref, wo_ref, wfi_ref, wfo_ref, g_ref, b_ref,
                        lq_ref, gs_ref, qs_ref, ks_ref, vs_ref, ck_hbm, cv_hbm,
                        o_ref, os_ref, kbuf, vbuf, sem, *, layer, lam_init):
    i = pl.program_id(0)
    n_steps = pl.num_programs(0)

    def seq_of(step, s):
        return jnp.minimum(step, DEC_BATCH // SEQ_PER_STEP - 1) * SEQ_PER_STEP + s

    def page_copies(step, s):
        seq = seq_of(step, s)
        cps = []
        for pg in range(N_PAGES):
            page = pt_ref[seq, pg]
            rows = pl.ds(pg * PAGE_ROWS, PAGE_ROWS)
            cps.append(pltpu.make_async_copy(ck_hbm.at[layer, page], kbuf.at[s, rows, :], sem.at[0, s]))
            cps.append(pltpu.make_async_copy(cv_hbm.at[layer, page], vbuf.at[s, rows, :], sem.at[1, s]))
        return cps

    @pl.when(i == 0)
    def _():
        for s in range(SEQ_PER_STEP):
            for cp in page_copies(0, s):
                cp.start()

    for s in range(SEQ_PER_STEP):
        for cp in page_copies(i, s):
            cp.wait()
    for s in range(SEQ_PER_STEP):
        os_ref[s * DEC_SEQ:(s + 1) * DEC_SEQ, :] = _sample_attend(
            kbuf.at[s], vbuf.at[s], qs_ref[...], ks_ref[...], vs_ref[...], s,
            lq_ref[...], gs_ref[...], lam_init)
    for s in range(SEQ_PER_STEP):
        for cp in page_copies(i + 1, s):
            cp.start()

    _tail_rows(h_ref, a_ref, p_ref, wo_ref, wfi_ref, wfo_ref, g_ref, b_ref, o_ref)

    @pl.when(i == n_steps - 1)
    def _():
        for s in range(SEQ_PER_STEP):
            for cp in page_copies(i + 1, s):
                cp.wait()


def _tail_sample(page_table, h, attn, pooled, wo_all, wfi_all, wfo_all, g_l, b_l,
                 lq_l, g_row, q_s, k_s, v_s, cache_k, cache_v, layer):
    n_rows = h.shape[0]
    n_steps = n_rows // TAIL_TILE
    n_seq_steps = DEC_BATCH // SEQ_PER_STEP
    assert n_steps >= n_seq_steps and DEC_BATCH % SEQ_PER_STEP == 0
    step_rows = SEQ_PER_STEP * DEC_SEQ
    kern = functools.partial(_tail_sample_kernel, layer=layer, lam_init=_lam_init(layer))
    row = lambda i, pt: (i, 0)
    const = lambda i, pt: (0, 0)
    seqs = lambda i, pt: (jnp.minimum(i, n_seq_steps - 1), 0)
    of_layer = lambda i, pt: (layer, 0, 0)
    once = pl.Buffered(1)
    grid_spec = pltpu.PrefetchScalarGridSpec(
        num_scalar_prefetch=1,
        grid=(n_steps,),
        in_specs=[
            pl.BlockSpec((TAIL_TILE, D_MODEL), row),
            pl.BlockSpec((TAIL_TILE, ATTN_WIDTH), row),
            pl.BlockSpec((TAIL_TILE, POOL_WIDTH), row),
            pl.BlockSpec((None, D_MODEL, D_MODEL), of_layer, pipeline_mode=once),
            pl.BlockSpec((None, D_MODEL, 2 * D_FF), of_layer, pipeline_mode=once),
            pl.BlockSpec((None, D_FF, D_MODEL), of_layer, pipeline_mode=once),
            pl.BlockSpec((2, D_MODEL), const),
            pl.BlockSpec((2, D_MODEL), const),
            pl.BlockSpec((4, HEAD_DIM), const),
            pl.BlockSpec((1, ATTN_WIDTH), const),
            pl.BlockSpec((step_rows, QK_WIDTH), seqs),
            pl.BlockSpec((step_rows, QK_WIDTH), seqs),
            pl.BlockSpec((step_rows, ATTN_WIDTH), seqs),
            pl.BlockSpec(memory_space=pl.ANY),
            pl.BlockSpec(memory_space=pl.ANY),
        ],
        out_specs=(
            pl.BlockSpec((TAIL_TILE, D_MODEL), row),
            pl.BlockSpec((step_rows, ATTN_WIDTH), seqs),
        ),
        scratch_shapes=[
            pltpu.VMEM((SEQ_PER_STEP, N_PAGES * PAGE_ROWS, 2 * HEAD_DIM), F32),
            pltpu.VMEM((SEQ_PER_STEP, N_PAGES * PAGE_ROWS, V_DIM), F32),
            pltpu.SemaphoreType.DMA((2, SEQ_PER_STEP)),
        ],
    )
    return pl.pallas_call(
        kern,
        out_shape=(
            jax.ShapeDtypeStruct((n_rows, D_MODEL), F32),
            jax.ShapeDtypeStruct((N_SAMPLE_ROWS, ATTN_WIDTH), F32),
        ),
        grid_spec=grid_spec,
        compiler_params=pltpu.CompilerParams(
            dimension_semantics=("arbitrary",),
            vmem_limit_bytes=VMEM_LIMIT_BYTES),
        name="tail_sample_attn",
    )(page_table, h, attn, pooled, wo_all, wfi_all, wfo_all, g_l, b_l,
      lq_l, g_row, q_s, k_s, v_s, cache_k, cache_v)


def kernel(x_prompt, x_sample, cache_k, cache_v, state_pool, page_table, meta_tokens,
           w_in, w_out, lambda_qk, subln_g, pool_w, pool_scale, w_ffn_in, w_ffn_out,
           ln_g, ln_b):
    assert x_prompt.shape == (BATCH, SEQ, D_MODEL)
    assert x_sample.shape == (DEC_BATCH, DEC_SEQ, D_MODEL)
    n_phys = cache_k.shape[1]
    ck = cache_k.reshape(DEPTH, n_phys, PAGE_ROWS, 2 * HEAD_DIM)
    cv = cache_v.reshape(DEPTH, n_phys, PAGE_ROWS, V_DIM)
    sp = state_pool.reshape(DEPTH, DEC_BATCH, POOL_HIST * POOL_WIDTH)

    w_in_b = w_in.astype(BF16)
    w_out_b = w_out.astype(BF16)
    w_ffn_in_b = w_ffn_in.astype(BF16)
    w_ffn_out_b = w_ffn_out.astype(BF16)
    pool_w_b = pool_w.astype(BF16)

    meta = jnp.broadcast_to(meta_tokens[None], (BATCH, N_META, D_MODEL))
    tail_pad = jnp.zeros((BATCH, L_PAD - L_PROMPT, D_MODEL), F32)
    hp = jnp.concatenate([meta, x_prompt, tail_pad], axis=1)
    hs = x_sample.reshape(N_SAMPLE_ROWS, D_MODEL)

    k_all = v_all = None
    u_p, k_s, v_s, u_s = [], [], [], []
    for l in range(DEPTH):
        pscale = pool_scale[l].reshape(1, POOL_WIDTH)
        g_col = subln_g[l].reshape(V_DIM, 1)
        g_row = jnp.tile(subln_g[l], N_HEADS).reshape(1, ATTN_WIDTH)

        k_all, v_all, kb, vt, qx_p, pooled, hist = _proj_prompt(
            hp, w_in_b, pool_w_b, pscale, k_all, v_all, l)
        attn = _prompt_attn(lambda_qk[l], g_col, qx_p, kb, vt, l)
        u_p.append(hist)

        kfs, vfs, qs, pooled_s, hist_s = _proj_sample(hs, w_in_b, pool_w_b, pscale, sp, l)
        hp, attn_s = _tail_sample(
            page_table, hp.reshape(BATCH * L_PAD, D_MODEL),
            attn.reshape(BATCH * L_PAD, ATTN_WIDTH), pooled.reshape(BATCH * L_PAD, POOL_WIDTH),
            w_out_b, w_ffn_in_b, w_ffn_out_b, ln_g[l], ln_b[l],
            lambda_qk[l], g_row, qs, kfs, vfs, ck, cv, l)
        hp = hp.reshape(BATCH, L_PAD, D_MODEL)
        hs = _tail(hs, attn_s, pooled_s, w_out_b, w_ffn_in_b, w_ffn_out_b, ln_g[l], ln_b[l], l)
        k_s.append(kfs)
        v_s.append(vfs)
        u_s.append(hist_s)

    def prompt_heads(x):
        return x.reshape(DEPTH, BATCH, L_PROMPT, N_HEADS, 2 * HEAD_DIM)

    def sample_heads(xs):
        return jnp.stack(xs).reshape(DEPTH, DEC_BATCH, DEC_SEQ, N_HEADS, 2 * HEAD_DIM)

    y_prompt = hp[:, N_META:L_PROMPT]
    y_sample = hs.reshape(DEC_BATCH, DEC_SEQ, D_MODEL)
    return (y_prompt, y_sample, prompt_heads(k_all), prompt_heads(v_all), jnp.stack(u_p),
            sample_heads(k_s), sample_heads(v_s),
            jnp.stack(u_s).reshape(DEPTH, DEC_BATCH, POOL_HIST, POOL_WIDTH))
```
